```python
import math
import jax, jax.numpy as jnp
from jax import lax
import numpy as np

D_MODEL = 2048
BATCH = 1
SEQ = 8192
DEPTH = 1
DEC_BATCH = 128
DEC_SEQ = 4
PAST_LEN = 2048
PAGE_SIZE = 128

N_HEADS = 16
KV_HEADS = 4
HEAD_DIM = D_MODEL // N_HEADS
IDX_HEADS = 16
IDX_DIM = 64
TOPK_MAX = 256
Q_BLOCK = 128
D_INNER = 2 * D_MODEL
SSM_HEAD_DIM = 64
SSM_HEADS = D_INNER // SSM_HEAD_DIM
SSM_GROUPS = 8
D_STATE = 128
CONV_W = 4
CONV_DIM = D_INNER + 2 * SSM_GROUPS * D_STATE
SSD_CHUNK = 128
D_FF = 4 * D_MODEL
ROPE_THETA = 10000.0
EPS = 1e-6
SPLITS = (N_HEADS * HEAD_DIM, KV_HEADS * HEAD_DIM, KV_HEADS * HEAD_DIM, IDX_HEADS * IDX_DIM, IDX_DIM, IDX_HEADS, D_INNER, CONV_DIM, SSM_HEADS, D_MODEL, D_MODEL)
IN_COLS = sum(SPLITS)

kernel_name = 'hybrid_dsa_ssd_gated_decode_step'


def _rms(xf):
    return xf * lax.rsqrt(jnp.mean(xf * xf, axis=-1, keepdims=True) + EPS)


def rmsnorm(x, g):
    return (_rms(x.astype(jnp.float32)) * g.astype(jnp.float32)).astype(x.dtype)


def layernorm(x, g, b):
    xf = x.astype(jnp.float32)
    xc = xf - jnp.mean(xf, axis=-1, keepdims=True)
    y = xc * lax.rsqrt(jnp.mean(xc * xc, axis=-1, keepdims=True) + EPS)
    return (y * g.astype(jnp.float32) + b.astype(jnp.float32)).astype(x.dtype)


def rope(x, pos):
    half = x.shape[-1] // 2
    inv = ROPE_THETA ** (-jnp.arange(half, dtype=jnp.float32) / half)
    ang = pos.astype(jnp.float32)[:, None] * inv[None, :]
    cos = jnp.cos(ang)[None, :, None, :]
    sin = jnp.sin(ang)[None, :, None, :]
    xf = x.astype(jnp.float32)
    x1, x2 = xf[..., :half], xf[..., half:]
    return jnp.concatenate([x1 * cos - x2 * sin, x2 * cos + x1 * sin], axis=-1).astype(x.dtype)


def gather_rows(rows, idx):
    return jax.vmap(lambda r, i: r[i])(rows, idx)


def mixer_inputs(x, pos, norm_mix_g, w_in, q_norm_g, k_norm_g, idx_ln_g, idx_ln_b):
    b, t = x.shape[:2]
    proj = rmsnorm(x, norm_mix_g) @ w_in
    cuts = [int(c) for c in np.cumsum(SPLITS)[:-1]]
    q, k, v, iq, ik, iw, z, xbc, dt, ga, gb = jnp.split(proj, cuts, axis=-1)
    q = rope(rmsnorm(q.reshape(b, t, N_HEADS, HEAD_DIM), q_norm_g), pos)
    k = rope(rmsnorm(k.reshape(b, t, KV_HEADS, HEAD_DIM), k_norm_g), pos)
    v = v.reshape(b, t, KV_HEADS, HEAD_DIM)
    iq = rope(iq.reshape(b, t, IDX_HEADS, IDX_DIM), pos)
    ik = rope(layernorm(ik, idx_ln_g, idx_ln_b)[:, :, None, :], pos)[:, :, 0, :]
    return q, k, v, iq, ik, iw, z, xbc, dt, ga, gb


def index_scores(iq, iw, ik, mask):
    dots = jnp.einsum('bthd,bsd->bths', iq.astype(jnp.float32), ik.astype(jnp.float32))
    wts = iw.astype(jnp.float32) * (IDX_HEADS ** -0.5 * IDX_DIM ** -0.5)
    score = jnp.einsum('bths,bth->bts', jax.nn.relu(dots), wts)
    return jnp.where(mask, score, -jnp.inf)


def sparse_attend(q, k_sel, v_sel, valid):
    b, t = q.shape[:2]
    qg = q.reshape(b, t, KV_HEADS, N_HEADS // KV_HEADS, HEAD_DIM).astype(jnp.float32)
    s = jnp.einsum('btgrd,btkgd->btgrk', qg, k_sel.astype(jnp.float32)) * (HEAD_DIM ** -0.5)
    s = jnp.where(valid[:, :, None, None, :], s, -jnp.inf)
    p = jax.nn.softmax(s, axis=-1)
    o = jnp.einsum('btgrk,btkgd->btgrd', p, v_sel.astype(jnp.float32))
    return o.reshape(b, t, N_HEADS * HEAD_DIM).astype(q.dtype)


def prompt_attention(q, k, v, iq, ik, iw):
    b, L = q.shape[:2]
    topk = min(TOPK_MAX, L // 4)
    nblk = L // Q_BLOCK

    def blocks(a):
        return jnp.moveaxis(a.reshape((b, nblk, Q_BLOCK) + a.shape[2:]), 1, 0)

    qpos = jnp.arange(L).reshape(nblk, Q_BLOCK)
    kpos = jnp.arange(L)

    def one_block(args):
        qb, iqb, iwb, pb = args
        mask = (kpos[None, :] <= pb[:, None])[None]
        _, sel = lax.top_k(index_scores(iqb, iwb, ik, mask), topk)
        valid = sel <= pb[None, :, None]
        return sparse_attend(qb, gather_rows(k, sel), gather_rows(v, sel), valid)

    out = lax.map(one_block, (blocks(q), blocks(iq), blocks(iw), qpos))
    return jnp.moveaxis(out, 0, 1).reshape(b, L, N_HEADS * HEAD_DIM)


def sample_attention(q, k_new, v_new, iq, ik_new, iw, cache_k, cache_v, cache_idx_k, page_table):
    db, t = q.shape[:2]
    L = PAST_LEN + t
    topk = min(TOPK_MAX, L // 4)
    ik_past = cache_idx_k[page_table].reshape(db, PAST_LEN, IDX_DIM)
    ik_all = jnp.concatenate([ik_past.astype(ik_new.dtype), ik_new], axis=1)
    qpos = PAST_LEN + jnp.arange(t)
    mask = (jnp.arange(L)[None, :] <= qpos[:, None])[None]
    _, sel = lax.top_k(index_scores(iq, iw, ik_all, mask), topk)
    valid = sel <= qpos[None, :, None]
    from_past = (sel < PAST_LEN)[..., None, None]
    pidx = jnp.minimum(sel, PAST_LEN - 1)
    phys = jnp.take_along_axis(page_table, (pidx // PAGE_SIZE).reshape(db, -1), axis=1).reshape(sel.shape)
    off = pidx % PAGE_SIZE
    nidx = jnp.clip(sel - PAST_LEN, 0, t - 1)
    k_sel = jnp.where(from_past, cache_k[phys, off].astype(k_new.dtype), gather_rows(k_new, nidx))
    v_sel = jnp.where(from_past, cache_v[phys, off].astype(v_new.dtype), gather_rows(v_new, nidx))
    return sparse_attend(q, k_sel, v_sel, valid)


def ssd_scan(x, dt, a, bm, cm, init):
    f32 = jnp.float32
    b, T = x.shape[:2]
    Q = math.gcd(T, SSD_CHUNK)
    c = T // Q
    G, R, P, N = SSM_GROUPS, SSM_HEADS // SSM_GROUPS, SSM_HEAD_DIM, D_STATE
    xdt = (x.astype(f32) * dt[..., None]).reshape(b, c, Q, G, R, P)
    da = jnp.moveaxis((dt * a).reshape(b, c, Q, G, R), 2, -1)
    cs = jnp.cumsum(da, axis=-1)
    bc = bm.astype(f32).reshape(b, c, Q, G, N)
    cc = cm.astype(f32).reshape(b, c, Q, G, N)
    causal = jnp.tril(jnp.ones((Q, Q), bool))
    lmat = jnp.exp(jnp.where(causal, cs[..., :, None] - cs[..., None, :], -jnp.inf))
    cb = jnp.einsum('bclgn,bcsgn->bcgls', cc, bc)
    y_diag = jnp.einsum('bcgrls,bcsgrp->bclgrp', cb[:, :, :, None] * lmat, xdt)
    decay_end = jnp.moveaxis(jnp.exp(cs[..., -1:] - cs), -1, 2)[..., None]
    chunk_states = jnp.einsum('bcsgn,bcsgrp->bcgrpn', bc, xdt * decay_end)
    states_all = jnp.concatenate([init.astype(f32).reshape(b, 1, G, R, P, N), chunk_states], axis=1)
    tot = jnp.moveaxis(cs[..., -1], 1, -1)
    cs_chunk = jnp.concatenate([jnp.zeros((b, G, R, 1), f32), jnp.cumsum(tot, axis=-1)], axis=-1)
    causal_c = jnp.tril(jnp.ones((c + 1, c + 1), bool))
    decay_chunk = jnp.exp(jnp.where(causal_c, cs_chunk[..., :, None] - cs_chunk[..., None, :], -jnp.inf))
    new_states = jnp.einsum('bgrzc,bcgrpn->bzgrpn', decay_chunk, states_all)
    states_in, final = new_states[:, :-1], new_states[:, -1]
    y_off = jnp.einsum('bclgn,bcgrpn->bclgrp', cc, states_in) * jnp.moveaxis(jnp.exp(cs), -1, 2)[..., None]
    y = (y_diag + y_off).reshape(b, T, SSM_HEADS, P)
    return y, final.reshape(b, SSM_HEADS, P, N)


def ssm_branch(z, xbc_raw, dt_raw, conv_state, ssm_state, conv_w, conv_b, dt_bias, a_log, d_skip, ssm_norm_g):
    f32 = jnp.float32
    b, t = xbc_raw.shape[:2]
    xpad = jnp.concatenate([conv_state.astype(xbc_raw.dtype), xbc_raw], axis=1)
    acc = xpad[:, 0:t] * conv_w[0]
    for i in range(1, CONV_W):
        acc = acc + xpad[:, i:i + t] * conv_w[i]
    xbc = jax.nn.silu(acc + conv_b)
    xs, bm, cm = jnp.split(xbc, [D_INNER, D_INNER + SSM_GROUPS * D_STATE], axis=-1)
    xs = xs.reshape(b, t, SSM_HEADS, SSM_HEAD_DIM)
    bm = bm.reshape(b, t, SSM_GROUPS, D_STATE)
    cm = cm.reshape(b, t, SSM_GROUPS, D_STATE)
    dt = jax.nn.softplus(dt_raw.astype(f32) + dt_bias.astype(f32))
    a = -jnp.exp(a_log.astype(f32))
    y, final = ssd_scan(xs, dt, a, bm, cm, ssm_state)
    y = y + d_skip.astype(f32)[:, None] * xs.astype(f32)
    y = y.reshape(b, t, D_INNER) * jax.nn.silu(z.astype(f32))
    y = _rms(y.reshape(b, t, SSM_GROUPS, D_INNER // SSM_GROUPS)).reshape(b, t, D_INNER) * ssm_norm_g.astype(f32)
    return y.astype(z.dtype), xpad[:, t:], final.astype(ssm_state.dtype)


def merge_and_mlp(x, attn_o, ssm_o, ga, gb, w_proj_attn, w_proj_ssm, w_out, norm_mlp_g, w_up, w_down):
    mixed = jax.nn.sigmoid(ga) * (attn_o @ w_proj_attn) + jax.nn.sigmoid(gb) * (ssm_o @ w_proj_ssm)
    h = x + mixed @ w_out
    u = rmsnorm(h, norm_mlp_g) @ w_up
    return h + jnp.square(jax.nn.relu(u)) @ w_down


def setup_inputs(seed: int = 0) -> dict:
    key = jax.random.key(seed)
    ks = jax.random.split(key, 28)
    f32 = jnp.float32
    n_pages = PAST_LEN // PAGE_SIZE
    n_used = DEC_BATCH * n_pages
    n_pool = n_used + max(1, n_used // 4)

    def nrm(k, shape, s):
        return s * jax.random.normal(k, shape, f32)

    page_table = jax.random.permutation(ks[0], n_pool)[:n_used].reshape(DEC_BATCH, n_pages).astype(jnp.int32)
    dt0 = jnp.exp(jax.random.uniform(ks[1], (DEPTH, SSM_HEADS), f32, math.log(1e-3), math.log(1e-1)))
    dt_bias = dt0 + jnp.log(-jnp.expm1(-dt0))
    a_log = jnp.log(jax.random.uniform(ks[2], (DEPTH, SSM_HEADS), f32, 1.0, 16.0))
    return {
        'x_prompt': nrm(ks[3], (BATCH, SEQ, D_MODEL), 1.0),
        'x_sample': nrm(ks[4], (DEC_BATCH, DEC_SEQ, D_MODEL), 1.0),
        'cache_k': nrm(ks[5], (DEPTH, n_pool, PAGE_SIZE, KV_HEADS, HEAD_DIM), 1.0),
        'cache_v': nrm(ks[6], (DEPTH, n_pool, PAGE_SIZE, KV_HEADS, HEAD_DIM), 1.0),
        'cache_idx_k': nrm(ks[7], (DEPTH, n_pool, PAGE_SIZE, IDX_DIM), 1.0),
        'state_conv': nrm(ks[8], (DEPTH, DEC_BATCH, CONV_W - 1, CONV_DIM), 1.0),
        'state_ssm': nrm(ks[9], (DEPTH, DEC_BATCH, SSM_HEADS, SSM_HEAD_DIM, D_STATE), 0.2),
        'page_table': page_table,
        'norm_mix_g': 1.0 + nrm(ks[10], (DEPTH, D_MODEL), 0.02),
        'w_in': nrm(ks[11], (DEPTH, D_MODEL, IN_COLS), D_MODEL ** -0.5),
        'q_norm_g': 1.0 + nrm(ks[12], (DEPTH, HEAD_DIM), 0.02),
        'k_norm_g': 1.0 + nrm(ks[13], (DEPTH, HEAD_DIM), 0.02),
        'idx_ln_g': 1.0 + nrm(ks[14], (DEPTH, IDX_DIM), 0.02),
        'idx_ln_b': nrm(ks[15], (DEPTH, IDX_DIM), 0.02),
        'conv_w': nrm(ks[16], (DEPTH, CONV_W, CONV_DIM), CONV_W ** -0.5),
        'conv_b': nrm(ks[17], (DEPTH, CONV_DIM), 0.01),
        'dt_bias': dt_bias,
        'a_log': a_log,
        'd_skip': 1.0 + nrm(ks[18], (DEPTH, SSM_HEADS), 0.1),
        'ssm_norm_g': 1.0 + nrm(ks[19], (DEPTH, D_INNER), 0.02),
        'w_proj_attn': nrm(ks[20], (DEPTH, N_HEADS * HEAD_DIM, D_MODEL), (N_HEADS * HEAD_DIM) ** -0.5),
        'w_proj_ssm': nrm(ks[21], (DEPTH, D_INNER, D_MODEL), D_INNER ** -0.5),
        'w_out': nrm(ks[22], (DEPTH, D_MODEL, D_MODEL), D_MODEL ** -0.5),
        'norm_mlp_g': 1.0 + nrm(ks[23], (DEPTH, D_MODEL), 0.02),
        'w_up': nrm(ks[24], (DEPTH, D_MODEL, D_FF), D_MODEL ** -0.5),
        'w_down': nrm(ks[25], (DEPTH, D_FF, D_MODEL), D_FF ** -0.5),
    }


def reference(x_prompt, x_sample, cache_k, cache_v, cache_idx_k, state_conv, state_ssm, page_table,
              norm_mix_g, w_in, q_norm_g, k_norm_g, idx_ln_g, idx_ln_b, conv_w, conv_b, dt_bias, a_log,
              d_skip, ssm_norm_g, w_proj_attn, w_proj_ssm, w_out, norm_mlp_g, w_up, w_down):
    b_p, seq = x_prompt.shape[:2]
    pos_p = jnp.arange(seq)
    pos_s = PAST_LEN + jnp.arange(x_sample.shape[1])
    hp, hs = x_prompt, x_sample
    kp, vp, ikp, cvp, ssp = [], [], [], [], []
    ks_, vs_, iks, cvs, sss = [], [], [], [], []
    for l in range(DEPTH):
        q, k, v, iq, ik, iw, z, xbc, dt, ga, gb = mixer_inputs(hp, pos_p, norm_mix_g[l], w_in[l], q_norm_g[l], k_norm_g[l], idx_ln_g[l], idx_ln_b[l])
        attn = prompt_attention(q, k, v, iq, ik, iw)
        zero_conv = jnp.zeros((b_p, CONV_W - 1, CONV_DIM), hp.dtype)
        zero_ssm = jnp.zeros((b_p, SSM_HEADS, SSM_HEAD_DIM, D_STATE), hp.dtype)
        ssm, conv_new, ssm_new = ssm_branch(z, xbc, dt, zero_conv, zero_ssm, conv_w[l], conv_b[l], dt_bias[l], a_log[l], d_skip[l], ssm_norm_g[l])
        hp = merge_and_mlp(hp, attn, ssm, ga, gb, w_proj_attn[l], w_proj_ssm[l], w_out[l], norm_mlp_g[l], w_up[l], w_down[l])
        kp.append(k); vp.append(v); ikp.append(ik); cvp.append(conv_new); ssp.append(ssm_new)
        q, k, v, iq, ik, iw, z, xbc, dt, ga, gb = mixer_inputs(hs, pos_s, norm_mix_g[l], w_in[l], q_norm_g[l], k_norm_g[l], idx_ln_g[l], idx_ln_b[l])
        attn = sample_attention(q, k, v, iq, ik, iw, cache_k[l], cache_v[l], cache_idx_k[l], page_table)
        ssm, conv_new, ssm_new = ssm_branch(z, xbc, dt, state_conv[l], state_ssm[l], conv_w[l], conv_b[l], dt_bias[l], a_log[l], d_skip[l], ssm_norm_g[l])
        hs = merge_and_mlp(hs, attn, ssm, ga, gb, w_proj_attn[l], w_proj_ssm[l], w_out[l], norm_mlp_g[l], w_up[l], w_down[l])
        ks_.append(k); vs_.append(v); iks.append(ik); cvs.append(conv_new); sss.append(ssm_new)
    return (hp, hs,
            jnp.stack(kp), jnp.stack(vp), jnp.stack(ikp), jnp.stack(cvp), jnp.stack(ssp),
            jnp.stack(ks_), jnp.stack(vs_), jnp.stack(iks), jnp.stack(cvs), jnp.stack(sss))
```

```python
import functools
import math

import jax
import jax.numpy as jnp
import numpy as np
from jax import lax
from jax.experimental import pallas as pl
from jax.experimental.pallas import tpu as pltpu

F32, BF16, I32 = jnp.float32, jnp.bfloat16, jnp.int32

D_MODEL = 2048
N_HEADS = 16
KV_HEADS = 4
HEAD_DIM = 128
Q_PER_KV = N_HEADS // KV_HEADS
IDX_HEADS = 16
IDX_DIM = 64
TOPK_MAX = 256
D_INNER = 4096
SSM_HEAD_DIM = 64
SSM_HEADS = 64
SSM_GROUPS = 8
D_STATE = 128
CONV_W = 4
CONV_DIM = D_INNER + 2 * SSM_GROUPS * D_STATE
SSD_CHUNK = 128
D_FF = 4 * D_MODEL
ROPE_THETA = 10000.0
EPS = 1e-6
SPLITS = (N_HEADS * HEAD_DIM, KV_HEADS * HEAD_DIM, KV_HEADS * HEAD_DIM, IDX_HEADS * IDX_DIM, IDX_DIM, IDX_HEADS,
          D_INNER, CONV_DIM, SSM_HEADS, D_MODEL, D_MODEL)

LANES = 128
SUBLANES = 8
VMEM_LIMIT = 56 * 1024 * 1024

C_Z = 0
C_Q = 4096
C_XBC = 6144
C_GA = 12288
C_GB = 14336
C_K = 16384
C_V = 16896
C_IQ = 17408
C_MISC = 18432
MISC_W = 512
N_PROJ = 19456

INT_MIN = -(2 ** 31)
INT_MAX = 2 ** 31 - 1
NEG_INF_KEY = -2139095041
NEG_BIG = -1e30
Q_TILE = 128
KEY_CHUNK = 512
LOG2E = 1.4426950408889634


def _cparams(*sem):
    return pltpu.CompilerParams(dimension_semantics=sem, vmem_limit_bytes=VMEM_LIMIT)


def _sortable_key(x):
    bits = pltpu.bitcast(x, I32)
    return bits ^ ((bits >> 31) & INT_MAX)


def _rmsnorm_kernel(x_ref, g_ref, o_ref):
    x = x_ref[...]
    ms = jnp.mean(x * x, axis=-1, keepdims=True)
    o_ref[...] = (x * lax.rsqrt(ms + EPS) * g_ref[...]).astype(o_ref.dtype)


def _rmsnorm(x, g, tm):
    m, d = x.shape
    return pl.pallas_call(
        _rmsnorm_kernel, grid=(m // tm,),
        in_specs=[pl.BlockSpec((tm, d), lambda i: (i, 0)), pl.BlockSpec((1, d), lambda i: (0, 0))],
        out_specs=pl.BlockSpec((tm, d), lambda i: (i, 0)),
        out_shape=jax.ShapeDtypeStruct((m, d), BF16),
        compiler_params=_cparams("parallel"), name="rmsnorm",
    )(x, g.reshape(1, d))


def _matmul_kernel(a_ref, w_ref, o_ref):
    o_ref[...] = jnp.dot(a_ref[...], w_ref[...], preferred_element_type=F32)


def _matmul(a, w, tm, tn):
    m, k = a.shape
    n = w.shape[1]
    return pl.pallas_call(
        _matmul_kernel, grid=(n // tn, m // tm),
        in_specs=[pl.BlockSpec((tm, k), lambda j, i: (i, 0)), pl.BlockSpec((k, tn), lambda j, i: (0, j))],
        out_specs=pl.BlockSpec((tm, tn), lambda j, i: (i, j)),
        out_shape=jax.ShapeDtypeStruct((m, n), F32),
        compiler_params=_cparams("parallel", "parallel"), name="in_proj",
    )(a, w)


def _mix_kernel(a_ref, s_ref, wa_ref, ws_ref, ga_ref, gb_ref, o_ref):
    pa = jnp.dot(a_ref[...], wa_ref[...], preferred_element_type=F32)
    ps = jnp.dot(s_ref[...], ws_ref[...], preferred_element_type=F32)
    o_ref[...] = (jax.nn.sigmoid(ga_ref[...]) * pa + jax.nn.sigmoid(gb_ref[...]) * ps).astype(o_ref.dtype)


def _mix(attn_o, ssm_o, wa, ws, proj, tm, tn):
    m = attn_o.shape[0]
    return pl.pallas_call(
        _mix_kernel, grid=(D_MODEL // tn, m // tm),
        in_specs=[pl.BlockSpec((tm, N_HEADS * HEAD_DIM), lambda j, i: (i, 0)),
                  pl.BlockSpec((tm, D_INNER), lambda j, i: (i, 0)),
                  pl.BlockSpec((N_HEADS * HEAD_DIM, tn), lambda j, i: (0, j)),
                  pl.BlockSpec((D_INNER, tn), lambda j, i: (0, j)),
                  pl.BlockSpec((tm, tn), lambda j, i: (i, C_GA // tn + j)),
                  pl.BlockSpec((tm, tn), lambda j, i: (i, C_GB // tn + j))],
        out_specs=pl.BlockSpec((tm, tn), lambda j, i: (i, j)),
        out_shape=jax.ShapeDtypeStruct((m, D_MODEL), BF16),
        compiler_params=_cparams("parallel", "parallel"), name="gated_merge",
    )(attn_o, ssm_o, wa, ws, proj, proj)


def _outproj_kernel(x_ref, mix_ref, w_ref, g_ref, h_ref, hn_ref):
    h = x_ref[...] + jnp.dot(mix_ref[...], w_ref[...], preferred_element_type=F32)
    h_ref[...] = h
    ms = jnp.mean(h * h, axis=-1, keepdims=True)
    hn_ref[...] = (h * lax.rsqrt(ms + EPS) * g_ref[...]).astype(hn_ref.dtype)


def _outproj(x, mixed, w, g, tm):
    m = x.shape[0]
    return pl.pallas_call(
        _outproj_kernel, grid=(m // tm,),
        in_specs=[pl.BlockSpec((tm, D_MODEL), lambda i: (i, 0)), pl.BlockSpec((tm, D_MODEL), lambda i: (i, 0)),
                  pl.BlockSpec((D_MODEL, D_MODEL), lambda i: (0, 0)), pl.BlockSpec((1, D_MODEL), lambda i: (0, 0))],
        out_specs=[pl.BlockSpec((tm, D_MODEL), lambda i: (i, 0)), pl.BlockSpec((tm, D_MODEL), lambda i: (i, 0))],
        out_shape=[jax.ShapeDtypeStruct((m, D_MODEL), F32), jax.ShapeDtypeStruct((m, D_MODEL), BF16)],
        compiler_params=_cparams("parallel"), name="out_proj",
    )(x, mixed, w, g.reshape(1, D_MODEL))


def _mlp_kernel(hn_ref, h_ref, wu_ref, wd_ref, o_ref):
    f = pl.program_id(1)
    u = jnp.dot(hn_ref[...], wu_ref[...], preferred_element_type=F32)
    act = jnp.square(jnp.maximum(u, 0.0)).astype(BF16)
    contrib = jnp.dot(act, wd_ref[...], preferred_element_type=F32)

    @pl.when(f == 0)
    def _():
        o_ref[...] = h_ref[...] + contrib

    @pl.when(f > 0)
    def _():
        o_ref[...] += contrib


def _mlp(hn, h, wu, wd, tm, tf):
    m = h.shape[0]
    return pl.pallas_call(
        _mlp_kernel, grid=(m // tm, D_FF // tf),
        in_specs=[pl.BlockSpec((tm, D_MODEL), lambda i, f: (i, 0)), pl.BlockSpec((tm, D_MODEL), lambda i, f: (i, 0)),
                  pl.BlockSpec((D_MODEL, tf), lambda i, f: (0, f)), pl.BlockSpec((tf, D_MODEL), lambda i, f: (f, 0))],
        out_specs=pl.BlockSpec((tm, D_MODEL), lambda i, f: (i, 0)),
        out_shape=jax.ShapeDtypeStruct((m, D_MODEL), F32),
        compiler_params=_cparams("parallel", "arbitrary"), name="mlp",
    )(hn, h, wu, wd)


def _prep_kernel(q_ref, k_ref, v_ref, iq_ref, misc_ref, ca_ref, sa_ref, cb_ref, sb_ref, qg_ref, kg_ref, lg_ref,
                 lb_ref, kf_ref, kb_ref, ikf_ref, ikb_ref, qn_ref, iqn_ref, qt_ref, iqt_ref, wt_ref, vt_ref):
    ca, sa, cb, sb = ca_ref[...], sa_ref[...], cb_ref[...], sb_ref[...]
    lane = lax.broadcasted_iota(I32, (Q_TILE, LANES), 1)
    first_half = (lane % IDX_DIM) < (IDX_DIM // 2)

    def norm_rope(x, g):
        ms = jnp.mean(x * x, axis=-1, keepdims=True)
        xn = x * lax.rsqrt(ms + EPS) * g
        return xn * ca + pltpu.roll(xn, HEAD_DIM // 2, 1) * sa

    def rope_idx(x):
        rot = jnp.where(first_half, pltpu.roll(x, LANES - IDX_DIM // 2, 1), pltpu.roll(x, IDX_DIM // 2, 1))
        return x * cb + rot * sb

    for h in range(N_HEADS):
        sl = slice(h * HEAD_DIM, (h + 1) * HEAD_DIM)
        qh = norm_rope(q_ref[:, sl], qg_ref[...])
        qn_ref[:, sl] = qh.astype(BF16)
        qt_ref[sl, :] = qh.T.astype(BF16)
    for h in range(KV_HEADS):
        sl = slice(h * HEAD_DIM, (h + 1) * HEAD_DIM)
        kh = norm_rope(k_ref[:, sl], kg_ref[...])
        kf_ref[:, sl] = kh
        kb_ref[:, sl] = kh.astype(BF16)
        vt_ref[sl, :] = v_ref[:, sl].T.astype(BF16)
    for c in range(IDX_HEADS * IDX_DIM // LANES):
        sl = slice(c * LANES, (c + 1) * LANES)
        x = rope_idx(iq_ref[:, sl])
        iqn_ref[:, sl] = x.astype(BF16)
        iqt_ref[sl, :] = x.T.astype(BF16)

    m = misc_ref[:, 0:LANES]
    is_k = lane < IDX_DIM
    mu = jnp.sum(jnp.where(is_k, m, 0.0), axis=-1, keepdims=True) * (1.0 / IDX_DIM)
    xc = jnp.where(is_k, m - mu, 0.0)
    var = jnp.sum(xc * xc, axis=-1, keepdims=True) * (1.0 / IDX_DIM)
    y = rope_idx(xc * lax.rsqrt(var + EPS) * lg_ref[...] + lb_ref[...])
    ikf_ref[...] = y[:, :IDX_DIM]
    ikb_ref[...] = y[:, :IDX_DIM].astype(BF16)
    wt_ref[...] = m.T[IDX_DIM:IDX_DIM + IDX_HEADS, :] * (IDX_HEADS ** -0.5 * IDX_DIM ** -0.5)


def _prep(proj, tabs, qg, kg, lg, lb):
    t = proj.shape[0]
    tq = Q_TILE
    row = lambda w, cb: pl.BlockSpec((tq, w), lambda i, cb=cb: (i, cb))
    const = lambda w: pl.BlockSpec((1, w), lambda i: (0, 0))
    colT = lambda r: pl.BlockSpec((r, tq), lambda i: (0, i))
    nq, nkv, niq = N_HEADS * HEAD_DIM, KV_HEADS * HEAD_DIM, IDX_HEADS * IDX_DIM
    outs = pl.pallas_call(
        _prep_kernel, grid=(t // tq,),
        in_specs=[row(nq, C_Q // nq), row(nkv, C_K // nkv), row(nkv, C_V // nkv), row(niq, C_IQ // niq),
                  row(MISC_W, C_MISC // MISC_W), row(LANES, 0), row(LANES, 0), row(LANES, 0), row(LANES, 0),
                  const(LANES), const(LANES), const(LANES), const(LANES)],
        out_specs=[row(nkv, 0), row(nkv, 0), row(IDX_DIM, 0), row(IDX_DIM, 0), row(nq, 0), row(niq, 0),
                   colT(nq), colT(niq), colT(IDX_HEADS), colT(nkv)],
        out_shape=[jax.ShapeDtypeStruct((t, nkv), F32), jax.ShapeDtypeStruct((t, nkv), BF16),
                   jax.ShapeDtypeStruct((t, IDX_DIM), F32), jax.ShapeDtypeStruct((t, IDX_DIM), BF16),
                   jax.ShapeDtypeStruct((t, nq), BF16), jax.ShapeDtypeStruct((t, niq), BF16),
                   jax.ShapeDtypeStruct((nq, t), BF16), jax.ShapeDtypeStruct((niq, t), BF16),
                   jax.ShapeDtypeStruct((IDX_HEADS, t), F32), jax.ShapeDtypeStruct((nkv, t), BF16)],
        compiler_params=_cparams("parallel"), name="norm_rope_prep",
    )(proj, proj, proj, proj, proj, *tabs, qg, kg, lg, lb)
    return dict(zip(("k_f32", "k_bf", "ik_f32", "ik_bf", "q_bf", "iq_bf", "qT", "iqT", "wT", "vT"), outs))


def _rope_tables(pos):
    def tab(dim):
        half = dim // 2
        inv = ROPE_THETA ** (-jnp.arange(half, dtype=F32) / half)
        ang = pos.astype(F32)[:, None] * inv[None, :]
        c, s = jnp.cos(ang), jnp.sin(ang)
        rep = LANES // dim
        return jnp.tile(jnp.concatenate([c, c], -1), (1, rep)), jnp.tile(jnp.concatenate([-s, s], -1), (1, rep))
    ca, sa = tab(HEAD_DIM)
    cb, sb = tab(IDX_DIM)
    return ca, sa, cb, sb


def _select_threshold(count_ge, count_gt, count_eq_before, topk, idx_bits, live):
    tau = jnp.full(live.shape, INT_MIN, I32)
    cand = tau ^ INT_MIN
    tau = jnp.where(count_ge(cand) >= topk, cand, tau)

    def bit_step(b, tau):
        cand = tau | lax.shift_left(jnp.int32(1), 30 - b)
        return jnp.where(count_ge(cand) >= topk, cand, tau)

    tau = lax.fori_loop(0, 31, bit_step, tau)
    n_gt = count_gt(tau)
    need = topk - n_gt
    n_eq = count_ge(tau) - n_gt
    tie = jnp.logical_and(jnp.logical_and(n_eq > need, tau != NEG_INF_KEY), live)
    has_tie = jnp.max(tie.astype(I32)) > 0

    def with_ties():
        def step(b, cut):
            cand = cut | lax.shift_left(jnp.int32(1), idx_bits - 1 - b)
            return jnp.where(count_eq_before(tau, cand) < need, cand, cut)
        return lax.fori_loop(0, idx_bits, step, jnp.zeros(live.shape, I32))

    cut = lax.cond(has_tie, with_ties, lambda: jnp.full(live.shape, INT_MAX, I32))
    return tau, cut


def _attn_prompt_kernel(ik_ref, k_ref, vt_ref, iqt_ref, wt_ref, qt_ref, o_ref, keys_ref, acc_ref, *, topk, idx_bits):
    i = pl.program_id(0)
    ch = KEY_CHUNK
    n_ch = (i * Q_TILE + Q_TILE + ch - 1) // ch
    t_glob = i * Q_TILE + lax.broadcasted_iota(I32, (1, Q_TILE), 1)
    row_iota = lax.broadcasted_iota(I32, (ch, Q_TILE), 0)

    def score_chunk(c, carry):
        base = pl.multiple_of(c * ch, ch)
        ikc = ik_ref[pl.ds(base, ch), :]
        acc = jnp.zeros((ch, Q_TILE), F32)
        for h in range(IDX_HEADS):
            d = jnp.dot(ikc, iqt_ref[h * IDX_DIM:(h + 1) * IDX_DIM, :], preferred_element_type=F32)
            acc = acc + jnp.maximum(d, 0.0) * wt_ref[h:h + 1, :]
        acc = jnp.where(base + row_iota <= t_glob, acc, -jnp.inf)
        keys_ref[pl.ds(base, ch), :] = _sortable_key(acc)
        return carry

    lax.fori_loop(0, n_ch, score_chunk, 0)

    def _count(pred):
        def body(c, cnt):
            base = pl.multiple_of(c * ch, ch)
            blk = keys_ref[pl.ds(base, ch), :]
            hit = pred(blk, base).astype(I32).reshape(ch // SUBLANES, SUBLANES, Q_TILE)
            return cnt + jnp.sum(hit, axis=0)
        cnt = lax.fori_loop(0, n_ch, body, jnp.zeros((SUBLANES, Q_TILE), I32))
        return jnp.sum(cnt, axis=0, keepdims=True)

    count_ge = lambda cand: _count(lambda blk, base: blk >= cand)
    count_gt = lambda cand: _count(lambda blk, base: blk > cand)
    count_eq_before = lambda tau, cut: _count(
        lambda blk, base: jnp.where(blk == tau, (base + row_iota < cut).astype(I32), 0))
    tau, cut = _select_threshold(count_ge, count_gt, count_eq_before, topk, idx_bits,
                                 jnp.ones((1, Q_TILE), jnp.bool_))

    def bias_chunk(c, carry):
        base = pl.multiple_of(c * ch, ch)
        blk = keys_ref[pl.ds(base, ch), :]
        idx = base + row_iota
        sel = jnp.where(blk == tau, (idx <= cut).astype(I32), (blk > tau).astype(I32))
        sel = jnp.where(idx <= t_glob, sel, 0)
        keys_ref[pl.ds(base, ch), :] = pltpu.bitcast(jnp.where(sel > 0, 0.0, NEG_BIG).astype(F32), I32)
        return carry

    lax.fori_loop(0, n_ch, bias_chunk, 0)

    scale2 = HEAD_DIM ** -0.5 * LOG2E
    for g in range(KV_HEADS):
        qg = jnp.concatenate([qt_ref[(g * Q_PER_KV + r) * HEAD_DIM:(g * Q_PER_KV + r + 1) * HEAD_DIM, :]
                              for r in range(Q_PER_KV)], axis=1)
        acc_ref[...] = jnp.zeros_like(acc_ref)

        def attn_chunk(c, carry):
            m, l = carry
            base = pl.multiple_of(c * ch, ch)
            kc = k_ref[pl.ds(base, ch), g * HEAD_DIM:(g + 1) * HEAD_DIM]
            bias = pltpu.bitcast(keys_ref[pl.ds(base, ch), :], F32)
            s = jnp.dot(kc, qg, preferred_element_type=F32) + jnp.concatenate([bias] * Q_PER_KV, axis=1)
            m_new = jnp.maximum(m, jnp.max(s, axis=0, keepdims=True))
            p = jnp.exp2((s - m_new) * scale2)
            alpha = jnp.exp2((m - m_new) * scale2)
            l = l * alpha + jnp.sum(p, axis=0, keepdims=True)
            pv = jnp.dot(vt_ref[g * HEAD_DIM:(g + 1) * HEAD_DIM, pl.ds(base, ch)], p.astype(BF16),
                         preferred_element_type=F32)
            acc_ref[...] = acc_ref[...] * alpha + pv
            return m_new, l

        m0 = jnp.full((1, Q_PER_KV * Q_TILE), NEG_BIG, F32)
        l0 = jnp.zeros((1, Q_PER_KV * Q_TILE), F32)
        _, l = lax.fori_loop(0, n_ch, attn_chunk, (m0, l0))
        o = acc_ref[...] / l
        for r in range(Q_PER_KV):
            h = g * Q_PER_KV + r
            o_ref[:, h * HEAD_DIM:(h + 1) * HEAD_DIM] = o[:, r * Q_TILE:(r + 1) * Q_TILE].T.astype(o_ref.dtype)


def _attn_prompt(p, topk):
    t = p["k_bf"].shape[0]
    tpad = -(-t // KEY_CHUNK) * KEY_CHUNK
    assert tpad == t, "prompt length must be a multiple of the key chunk"
    nq, nkv, niq = N_HEADS * HEAD_DIM, KV_HEADS * HEAD_DIM, IDX_HEADS * IDX_DIM
    full = lambda shape: pl.BlockSpec(shape, lambda i: (0, 0))
    colT = lambda r: pl.BlockSpec((r, Q_TILE), lambda i: (0, i))
    kern = functools.partial(_attn_prompt_kernel, topk=topk, idx_bits=max(1, (t - 1).bit_length()))
    return pl.pallas_call(
        kern, grid=(t // Q_TILE,),
        in_specs=[full((t, IDX_DIM)), full((t, nkv)), full((nkv, t)), colT(niq), colT(IDX_HEADS), colT(nq)],
        out_specs=pl.BlockSpec((Q_TILE, nq), lambda i: (i, 0)),
        out_shape=jax.ShapeDtypeStruct((t, nq), BF16),
        scratch_shapes=[pltpu.VMEM((t, Q_TILE), I32), pltpu.VMEM((HEAD_DIM, Q_PER_KV * Q_TILE), F32)],
        compiler_params=_cparams("arbitrary"), name="prompt_attention",
    )(p["ik_bf"], p["k_bf"], p["vT"], p["iqT"], p["wT"], p["qT"])


def _softplus(x):
    return jnp.maximum(x, 0.0) + jnp.log1p(jnp.exp(-jnp.abs(x)))


def _split3(v):
    v1 = v.astype(BF16)
    r1 = v - v1.astype(F32)
    v2 = r1.astype(BF16)
    v3 = (r1 - v2.astype(F32)).astype(BF16)
    return v1, v2, v3


def _dot3(parts, e):
    return sum(jnp.dot(p, e, preferred_element_type=F32) for p in parts)


LANE_BLK = 512


def _conv_silu(ext_ref, row0, rows, cw_ref, cb_ref, xa_ref):
    for j in range(CONV_DIM // LANE_BLK):
        sl = slice(j * LANE_BLK, (j + 1) * LANE_BLK)
        acc = ext_ref[row0:row0 + rows, sl] * cw_ref[0:1, sl]
        for i in range(1, CONV_W):
            acc = acc + ext_ref[row0 + i:row0 + i + rows, sl] * cw_ref[i:i + 1, sl]
        acc = acc + cb_ref[:, sl]
        xa_ref[:, sl] = acc * jax.nn.sigmoid(acc)


def _gated_norm(y_ref, xa_ref, z_ref, dsk_ref, ng_ref, o_ref, zidx):
    for j in range(SSM_GROUPS):
        sl = slice(j * LANE_BLK, (j + 1) * LANE_BLK)
        y = y_ref[:, sl] + dsk_ref[:, sl] * xa_ref[:, sl]
        zz = z_ref[zidx + (slice(None), sl)] if zidx else z_ref[:, sl]
        y = y * (zz * jax.nn.sigmoid(zz))
        ms = jnp.mean(y * y, axis=-1, keepdims=True)
        res = (y * lax.rsqrt(ms + EPS) * ng_ref[:, sl]).astype(o_ref.dtype)
        if zidx:
            o_ref[zidx + (slice(None), sl)] = res
        else:
            o_ref[:, sl] = res


def _ssd_prompt_kernel(xbc_ref, z_ref, misc_ref, cw_ref, cb_ref, dtb_ref, alog_ref, dsk_ref, ng_ref, exp_ref, tri_ref,
                       o_ref, st_ref, ext_ref, xa_ref, xlo_ref, xhi_ref, xw_ref, ecs_ref, y_ref, state_ref):
    c = pl.program_id(0)
    q = SSD_CHUNK
    pad = SUBLANES

    @pl.when(c == 0)
    def _():
        ext_ref[0:pad, :] = jnp.zeros((pad, CONV_DIM), F32)
        state_ref[...] = jnp.zeros_like(state_ref)

    ext_ref[pad:pad + q, :] = xbc_ref[...]
    _conv_silu(ext_ref, pad - (CONV_W - 1), q, cw_ref, cb_ref, xa_ref)
    ext_ref[0:pad, :] = ext_ref[q:q + pad, :]

    dt = _softplus(misc_ref[:, LANES:2 * LANES] + dtb_ref[...])
    da = dt * (-jnp.exp(alog_ref[...]))
    cs = jnp.dot(tri_ref[...], da, precision=lax.Precision.HIGHEST, preferred_element_type=F32)
    cs_t = cs.T
    ecs = jnp.exp(cs)
    dec_end = jnp.exp(cs[q - 1:q, :] - cs)
    parts = _split3(jnp.concatenate([dt, dt * dec_end, ecs], axis=0))
    lane = lax.broadcasted_iota(I32, (q, LANE_BLK), 1)
    lo = (lane % LANES) < SSM_HEAD_DIM
    for j in range(D_INNER // LANE_BLK):
        sl = slice(j * LANE_BLK, (j + 1) * LANE_BLK)
        ex = _dot3(parts, exp_ref[:, sl])
        xs = xa_ref[:, sl]
        xdt = xs * ex[0:q]
        xlo_ref[:, sl] = jnp.where(lo, xdt, 0.0).astype(BF16)
        xhi_ref[:, sl] = jnp.where(lo, 0.0, xdt).astype(BF16)
        xw_ref[:, sl] = (xs * ex[q:2 * q]).astype(BF16)
        ecs_ref[:, sl] = ex[2 * q:3 * q]

    causal = lax.broadcasted_iota(I32, (q, q), 0) >= lax.broadcasted_iota(I32, (q, q), 1)
    heads_per_group = SSM_HEADS // SSM_GROUPS
    for g in range(SSM_GROUPS):
        b0 = D_INNER + g * D_STATE
        c0 = D_INNER + SSM_GROUPS * D_STATE + g * D_STATE
        bg = xa_ref[:, b0:b0 + D_STATE]
        cg = xa_ref[:, c0:c0 + D_STATE].astype(BF16)
        cbm = lax.dot_general(cg, bg.astype(BF16), (((1,), (1,)), ((), ())), preferred_element_type=F32)
        bg_t = bg.T.astype(BF16)
        for j in range(heads_per_group // 2):
            pidx = g * (heads_per_group // 2) + j
            sl = slice(pidx * LANES, (pidx + 1) * LANES)
            yd = None
            for hh, x_ref in ((0, xlo_ref), (1, xhi_ref)):
                h = 2 * pidx + hh
                lm = jnp.exp(jnp.where(causal, cs[:, h:h + 1] - cs_t[h:h + 1, :], -jnp.inf))
                t = jnp.dot((cbm * lm).astype(BF16), x_ref[:, sl], preferred_element_type=F32)
                yd = t if yd is None else yd + t
            st = state_ref[pidx]
            yo = jnp.dot(cg, st.astype(BF16), preferred_element_type=F32) * ecs_ref[:, sl]
            y_ref[:, sl] = yd + yo
            state_ref[pidx] = st * ecs_ref[q - 1:q, sl] + jnp.dot(bg_t, xw_ref[:, sl], preferred_element_type=F32)

    _gated_norm(y_ref, xa_ref, z_ref, dsk_ref, ng_ref, o_ref, ())

    @pl.when(c == pl.num_programs(0) - 1)
    def _():
        for p in range(SSM_HEADS // 2):
            st_ref[2 * p:2 * p + 2] = state_ref[p].T.reshape(2, SSM_HEAD_DIM, D_STATE)


def _ssm_consts(conv_w, conv_b, dt_bias, a_log, d_skip, ssm_norm_g):
    pad_h = lambda v: jnp.pad(v.reshape(1, SSM_HEADS), ((0, 0), (0, LANES - SSM_HEADS)))
    head_of_lane = np.arange(D_INNER) // SSM_HEAD_DIM
    expand = jnp.asarray((np.arange(LANES)[:, None] == head_of_lane[None, :]), BF16)
    return dict(cw=conv_w, cb=conv_b.reshape(1, CONV_DIM), dtb=pad_h(dt_bias), alog=pad_h(a_log),
                dsk=jnp.repeat(d_skip, SSM_HEAD_DIM).reshape(1, D_INNER), ng=ssm_norm_g.reshape(1, D_INNER),
                expand=expand)


def _ssd_prompt(proj, sc):
    t = proj.shape[0]
    q = SSD_CHUNK
    assert t % q == 0
    tri = jnp.asarray(np.tril(np.ones((q, q), np.float32)))
    row = lambda w, cb: pl.BlockSpec((q, w), lambda i, cb=cb: (i, cb))
    const = lambda r, w: pl.BlockSpec((r, w), lambda i: (0, 0))
    return pl.pallas_call(
        _ssd_prompt_kernel, grid=(t // q,),
        in_specs=[row(CONV_DIM, C_XBC // CONV_DIM), row(D_INNER, C_Z // D_INNER), row(MISC_W, C_MISC // MISC_W),
                  const(CONV_W, CONV_DIM), const(1, CONV_DIM), const(1, LANES), const(1, LANES),
                  const(1, D_INNER), const(1, D_INNER), const(LANES, D_INNER), const(q, q)],
        out_specs=[pl.BlockSpec((q, D_INNER), lambda i: (i, 0)),
                   pl.BlockSpec((SSM_HEADS, SSM_HEAD_DIM, D_STATE), lambda i: (0, 0, 0))],
        out_shape=[jax.ShapeDtypeStruct((t, D_INNER), BF16),
                   jax.ShapeDtypeStruct((SSM_HEADS, SSM_HEAD_DIM, D_STATE), F32)],
        scratch_shapes=[pltpu.VMEM((q + 2 * SUBLANES, CONV_DIM), F32), pltpu.VMEM((q, CONV_DIM), F32),
                        pltpu.VMEM((q, D_INNER), BF16), pltpu.VMEM((q, D_INNER), BF16), pltpu.VMEM((q, D_INNER), BF16),
                        pltpu.VMEM((q, D_INNER), F32), pltpu.VMEM((q, D_INNER), F32),
                        pltpu.VMEM((SSM_HEADS // 2, D_STATE, LANES), F32)],
        compiler_params=_cparams("arbitrary"), name="ssd_prompt",
    )(proj, proj, proj, sc["cw"], sc["cb"], sc["dtb"], sc["alog"], sc["dsk"], sc["ng"], sc["expand"], tri)


def _ssd_sample_kernel(xbc_ref, z_ref, misc_ref, cst_ref, st_ref, cw_ref, cb_ref, dtb_ref, alog_ref, dsk_ref, ng_ref,
                       exp_ref, gsum_ref, gexp_ref, o_ref, nst_ref, ext_ref, xa_ref, y_ref, *, t):
    nt = (((1,), (1,)), ((), ()))
    tn = (((0,), (0,)), ((), ()))
    ext_ref[...] = jnp.zeros_like(ext_ref)
    ext_ref[0:CONV_W - 1, :] = cst_ref[0]
    ext_ref[CONV_W - 1:CONV_W - 1 + t, :] = xbc_ref[0]
    _conv_silu(ext_ref, 0, t, cw_ref, cb_ref, xa_ref)

    dt = _softplus(misc_ref[0][:, LANES:2 * LANES] + dtb_ref[...])
    da = dt * (-jnp.exp(alog_ref[...]))
    rows = [da[0:1]]
    for i in range(1, t):
        rows.append(rows[-1] + da[i:i + 1])
    cs = jnp.concatenate(rows, axis=0)
    ecs = jnp.exp(cs)
    dec_end = jnp.exp(cs[t - 1:t] - cs)
    parts = _split3(jnp.concatenate([dt, dt * dec_end, ecs, cs], axis=0))

    pairs = [(l, s) for l in range(t) for s in range(l + 1)]
    n_b = SSM_GROUPS * D_STATE
    bm = xa_ref[:, D_INNER:D_INNER + n_b]
    cm = xa_ref[:, D_INNER + n_b:D_INNER + 2 * n_b]
    prod = jnp.concatenate([cm[l:l + 1] * bm[s:s + 1] for l, s in pairs], axis=0)
    cb_parts = _split3(_dot3(_split3(prod), gsum_ref[...]))

    heads_per_group = SSM_HEADS // SSM_GROUPS
    for g in range(SSM_GROUPS):
        sl = slice(g * LANE_BLK, (g + 1) * LANE_BLK)
        ex = _dot3(parts, exp_ref[:, sl])
        cbe = _dot3(cb_parts, gexp_ref[:, sl])
        xs = xa_ref[:, sl]
        xdt, xw, ecs_e, cs_e = xs * ex[0:t], xs * ex[t:2 * t], ex[2 * t:3 * t], ex[3 * t:4 * t]
        yrows = []
        for l in range(t):
            acc = None
            for s in range(l + 1):
                pi = pairs.index((l, s))
                coef = cbe[pi:pi + 1] if s == l else cbe[pi:pi + 1] * jnp.exp(cs_e[l:l + 1] - cs_e[s:s + 1])
                term = coef * xdt[s:s + 1]
                acc = term if acc is None else acc + term
            yrows.append(acc)
        sg = st_ref[0, g * heads_per_group:(g + 1) * heads_per_group].reshape(LANE_BLK, D_STATE)
        cg = xa_ref[:, D_INNER + n_b + g * D_STATE:D_INNER + n_b + (g + 1) * D_STATE].astype(BF16)
        yo = lax.dot_general(cg, sg.astype(BF16), nt, preferred_element_type=F32) * ecs_e
        y_ref[:, sl] = jnp.concatenate(yrows, axis=0) + yo
        bg = xa_ref[:, D_INNER + g * D_STATE:D_INNER + (g + 1) * D_STATE]
        zpad = jnp.zeros((SUBLANES - t, LANE_BLK), F32)
        upd = lax.dot_general(jnp.concatenate([xw, zpad], axis=0).astype(BF16),
                              jnp.concatenate([bg, zpad[:, :D_STATE]], axis=0).astype(BF16), tn,
                              preferred_element_type=F32)
        for r in range(heads_per_group):
            h = g * heads_per_group + r
            nst_ref[0, h] = st_ref[0, h] * ecs[t - 1:t, h:h + 1] + upd[r * SSM_HEAD_DIM:(r + 1) * SSM_HEAD_DIM]

    _gated_norm(y_ref, xa_ref, z_ref, dsk_ref, ng_ref, o_ref, (0,))


def _ssd_sample(proj3, conv_state, ssm_state, sc):
    b, t = proj3.shape[:2]
    assert t + CONV_W - 1 <= SUBLANES
    n_pairs = t * (t + 1) // 2
    n_b = SSM_GROUPS * D_STATE
    gsum = jnp.asarray(np.arange(n_b)[:, None] // D_STATE == np.arange(LANES)[None, :], BF16)
    gexp = jnp.asarray(np.arange(LANES)[:, None] == (np.arange(D_INNER) // LANE_BLK)[None, :], BF16)
    row = lambda w, cb: pl.BlockSpec((1, t, w), lambda i, cb=cb: (i, 0, cb))
    const = lambda r, w: pl.BlockSpec((r, w), lambda i: (0, 0))
    st_spec = pl.BlockSpec((1, SSM_HEADS, SSM_HEAD_DIM, D_STATE), lambda i: (i, 0, 0, 0))
    kern = functools.partial(_ssd_sample_kernel, t=t)
    del n_pairs
    return pl.pallas_call(
        kern, grid=(b,),
        in_specs=[row(CONV_DIM, C_XBC // CONV_DIM), row(D_INNER, C_Z // D_INNER), row(MISC_W, C_MISC // MISC_W),
                  pl.BlockSpec((1, CONV_W - 1, CONV_DIM), lambda i: (i, 0, 0)), st_spec,
                  const(CONV_W, CONV_DIM), const(1, CONV_DIM), const(1, LANES), const(1, LANES),
                  const(1, D_INNER), const(1, D_INNER), const(LANES, D_INNER), const(n_b, LANES),
                  const(LANES, D_INNER)],
        out_specs=[pl.BlockSpec((1, t, D_INNER), lambda i: (i, 0, 0)), st_spec],
        out_shape=[jax.ShapeDtypeStruct((b, t, D_INNER), BF16),
                   jax.ShapeDtypeStruct((b, SSM_HEADS, SSM_HEAD_DIM, D_STATE), F32)],
        scratch_shapes=[pltpu.VMEM((SUBLANES, CONV_DIM), F32), pltpu.VMEM((t, CONV_DIM), F32),
                        pltpu.VMEM((t, D_INNER), F32)],
        compiler_params=_cparams("parallel"), name="ssd_sample",
    )(proj3, proj3, proj3, conv_state, ssm_state, sc["cw"], sc["cb"], sc["dtb"], sc["alog"], sc["dsk"], sc["ng"],
      sc["expand"], gsum, gexp)


SEQS_PER_STEP = 8
PAGES_PER_STEP = 8
Q_SLOTS = SUBLANES
NT_DIMS = (((1,), (1,)), ((), ()))


def _scores_sample_kernel(pt_ref, *refs, past, t_new, topk, idx_bits):
    del pt_ref
    gs = SEQS_PER_STEP
    page_refs = refs[:gs]
    iq_ref, w_ref, ikn_ref, bias_ref, keys_ref = refs[gs:]
    j = pl.program_id(1)
    lp = past + LANES

    def head_sum(iq, w, keys_bf):
        d = lax.dot_general(iq, keys_bf, NT_DIMS, preferred_element_type=F32)
        s = jnp.maximum(d, 0.0) * w
        return jnp.sum(s.reshape(IDX_HEADS, Q_SLOTS, LANES), axis=0)

    for k in range(gs):
        s = head_sum(iq_ref[k], w_ref[k], page_refs[k][0].astype(BF16))
        keys_ref[k, :, pl.ds(pl.multiple_of(j * LANES, LANES), LANES)] = _sortable_key(s)

    @pl.when(j == pl.num_programs(1) - 1)
    def _():
        tq = lax.broadcasted_iota(I32, (Q_SLOTS, LANES), 0)
        jn = lax.broadcasted_iota(I32, (Q_SLOTS, LANES), 1)
        ok_new = jnp.logical_and(jn <= tq, jn < t_new)
        for k in range(gs):
            s = head_sum(iq_ref[k], w_ref[k], ikn_ref[k])
            keys_ref[k, :, past:lp] = _sortable_key(jnp.where(ok_new, s, -jnp.inf))

        idx = lax.broadcasted_iota(I32, (gs, Q_SLOTS, lp), 2)
        slot = lax.broadcasted_iota(I32, (gs, Q_SLOTS, lp), 1)
        cnt = lambda hit: jnp.sum(hit.astype(I32), axis=2, keepdims=True)
        count_ge = lambda cand: cnt(keys_ref[...] >= cand)
        count_gt = lambda cand: cnt(keys_ref[...] > cand)
        count_eq_before = lambda tau, cutc: cnt(jnp.where(keys_ref[...] == tau, (idx < cutc).astype(I32), 0))
        live = lax.broadcasted_iota(I32, (gs, Q_SLOTS, 1), 1) < t_new
        tau, cut = _select_threshold(count_ge, count_gt, count_eq_before, topk, idx_bits, live)
        keys = keys_ref[...]
        sel = jnp.where(keys == tau, (idx <= cut).astype(I32), (keys > tau).astype(I32))
        new_ok = jnp.where(idx - past <= slot, (idx - past < t_new).astype(I32), 0)
        valid = jnp.where(idx < past, 1, new_ok)
        bias_ref[...] = jnp.where(sel * valid > 0, 0.0, NEG_BIG).astype(F32)


def _scores_sample(page_table, cache_idx, iq_rows, w_rows, ikn, t_new, topk):
    b, n_pages = page_table.shape
    page = cache_idx.shape[1]
    assert page == LANES and b % SEQS_PER_STEP == 0
    past = n_pages * page
    lp = past + LANES
    gs = SEQS_PER_STEP
    page_spec = lambda k: pl.BlockSpec((1, page, IDX_DIM), lambda sg, j, pt, k=k: (pt[sg * gs + k, j], 0, 0))
    grp = lambda r, w: pl.BlockSpec((gs, r, w), lambda sg, j, pt: (sg, 0, 0))
    rows = IDX_HEADS * Q_SLOTS
    kern = functools.partial(_scores_sample_kernel, past=past, t_new=t_new, topk=topk,
                             idx_bits=max(1, (lp - 1).bit_length()))
    return pl.pallas_call(
        kern,
        grid_spec=pltpu.PrefetchScalarGridSpec(
            num_scalar_prefetch=1, grid=(b // gs, n_pages),
            in_specs=[page_spec(k) for k in range(gs)] + [grp(rows, IDX_DIM), grp(rows, LANES), grp(LANES, IDX_DIM)],
            out_specs=grp(Q_SLOTS, lp),
            scratch_shapes=[pltpu.VMEM((gs, Q_SLOTS, lp), I32)]),
        out_shape=jax.ShapeDtypeStruct((b, Q_SLOTS, lp), F32),
        compiler_params=_cparams("parallel", "arbitrary"), name="sample_index_select",
    )(page_table, *([cache_idx] * gs), iq_rows, w_rows, ikn)


def _attn_sample_kernel(pt_ref, *refs, pp):
    del pt_ref
    k_refs, v_refs = refs[:pp], refs[pp:2 * pp]
    kn_ref, vn_ref, q_ref, bias_ref, biasn_ref, o_ref, m_ref, l_ref, acc_ref = refs[2 * pp:]
    s_id = pl.program_id(1)
    scale2 = HEAD_DIM ** -0.5 * LOG2E
    rows = Q_PER_KV * Q_SLOTS

    @pl.when(s_id == 0)
    def _():
        m_ref[...] = jnp.full(m_ref.shape, NEG_BIG, F32)
        l_ref[...] = jnp.zeros_like(l_ref)
        acc_ref[...] = jnp.zeros_like(acc_ref)

    def process(kpage, vpage, bias):
        bias_r = jnp.concatenate([bias] * Q_PER_KV, axis=0)
        kb, vb = kpage.astype(BF16), vpage.astype(BF16)
        for g in range(KV_HEADS):
            sl = slice(g * HEAD_DIM, (g + 1) * HEAD_DIM)
            sc = lax.dot_general(q_ref[0, g], kb[:, sl], NT_DIMS, preferred_element_type=F32) + bias_r
            m_prev = m_ref[g]
            m_new = jnp.maximum(m_prev, jnp.max(sc, axis=-1, keepdims=True))
            p = jnp.exp2((sc - m_new) * scale2)
            alpha = jnp.exp2((m_prev - m_new) * scale2)
            l_ref[g] = l_ref[g] * alpha + jnp.sum(p, axis=-1, keepdims=True)
            acc_ref[g] = acc_ref[g] * alpha + jnp.dot(p.astype(BF16), vb[:, sl], preferred_element_type=F32)
            m_ref[g] = m_new

    for pidx in range(pp):
        process(k_refs[pidx][0], v_refs[pidx][0], bias_ref[0, :, pidx * LANES:(pidx + 1) * LANES])

    @pl.when(s_id == pl.num_programs(1) - 1)
    def _():
        process(kn_ref[0], vn_ref[0], biasn_ref[0])
        o_ref[0] = (acc_ref[...] / l_ref[...]).astype(o_ref.dtype)
    del rows


def _attn_sample(page_table, cache_k, cache_v, kn, vn, q_rows, bias):
    b, n_pages = page_table.shape
    page = cache_k.shape[1]
    pp = min(PAGES_PER_STEP, n_pages)
    assert n_pages % pp == 0 and page == LANES
    nkv = KV_HEADS * HEAD_DIM
    rows = Q_PER_KV * Q_SLOTS
    page_spec = lambda k: pl.BlockSpec((1, page, nkv), lambda i, s, pt, k=k: (pt[i, s * pp + k], 0, 0))
    per_seq = lambda shape: pl.BlockSpec((1,) + shape, lambda i, s, pt: (i,) + (0,) * len(shape))
    kern = functools.partial(_attn_sample_kernel, pp=pp)
    return pl.pallas_call(
        kern,
        grid_spec=pltpu.PrefetchScalarGridSpec(
            num_scalar_prefetch=1, grid=(b, n_pages // pp),
            in_specs=[page_spec(k) for k in range(pp)] * 2 + [
                per_seq((LANES, nkv)), per_seq((LANES, nkv)), per_seq((KV_HEADS, rows, HEAD_DIM)),
                pl.BlockSpec((1, Q_SLOTS, pp * LANES), lambda i, s, pt: (i, 0, s)),
                pl.BlockSpec((1, Q_SLOTS, LANES), lambda i, s, pt: (i, 0, n_pages))],
            out_specs=per_seq((KV_HEADS, rows, HEAD_DIM)),
            scratch_shapes=[pltpu.VMEM((KV_HEADS, rows, LANES), F32), pltpu.VMEM((KV_HEADS, rows, LANES), F32),
                            pltpu.VMEM((KV_HEADS, rows, HEAD_DIM), F32)]),
        out_shape=jax.ShapeDtypeStruct((b, KV_HEADS, rows, HEAD_DIM), BF16),
        compiler_params=_cparams("parallel", "arbitrary"), name="sample_attention",
    )(page_table, *([cache_k] * pp), *([cache_v] * pp), kn, vn, q_rows, bias, bias)


def _pad_rows(a, axis, size):
    widths = [(0, 0)] * a.ndim
    widths[axis] = (0, size - a.shape[axis])
    return jnp.pad(a, widths)


def _trunk(x, proj, attn_o, ssm_o, w):
    m = x.shape[0]
    tm = min(512, m)
    mixed = _mix(attn_o, ssm_o, w["pa"], w["ps"], proj, tm, 512)
    h, hn = _outproj(x, mixed, w["out"], w["norm_mlp_g"], min(256, m))
    return _mlp(hn, h, w["up"], w["down"], tm, 1024)


def kernel(x_prompt, x_sample, cache_k, cache_v, cache_idx_k, state_conv, state_ssm, page_table, norm_mix_g, w_in,
           q_norm_g, k_norm_g, idx_ln_g, idx_ln_b, conv_w, conv_b, dt_bias, a_log, d_skip, ssm_norm_g, w_proj_attn,
           w_proj_ssm, w_out, norm_mlp_g, w_up, w_down):
    depth = w_in.shape[0]
    bp, seq = x_prompt.shape[:2]
    db, t_new = x_sample.shape[:2]
    assert depth == 1 and bp == 1, "single-layer, single-prompt configuration"
    n_pages = page_table.shape[1]
    page = cache_k.shape[2]
    past = n_pages * page
    nkv = KV_HEADS * HEAD_DIM
    l = 0

    cuts = [int(c) for c in np.cumsum(SPLITS)[:-1]]
    qc, kc, vc, iqc, ikc, iwc, zc, xbcc, dtc, gac, gbc = jnp.split(w_in[l], cuts, axis=1)
    zeros = lambda n: jnp.zeros((D_MODEL, n), w_in.dtype)
    misc = jnp.concatenate([ikc, iwc, zeros(LANES - IDX_DIM - IDX_HEADS), dtc, zeros(MISC_W - LANES - SSM_HEADS)], axis=1)
    w_p = jnp.concatenate([zc, qc, xbcc, gac, gbc, kc, vc, iqc, misc, zeros(N_PROJ - C_MISC - MISC_W)], axis=1).astype(BF16)
    w = dict(pa=w_proj_attn[l].astype(BF16), ps=w_proj_ssm[l].astype(BF16), out=w_out[l].astype(BF16),
             up=w_up[l].astype(BF16), down=w_down[l].astype(BF16), norm_mlp_g=norm_mlp_g[l])
    sc = _ssm_consts(conv_w[l], conv_b[l], dt_bias[l], a_log[l], d_skip[l], ssm_norm_g[l])
    qg, kg = q_norm_g[l].reshape(1, HEAD_DIM), k_norm_g[l].reshape(1, HEAD_DIM)
    lg = _pad_rows(idx_ln_g[l].reshape(1, IDX_DIM), 1, LANES)
    lb = _pad_rows(idx_ln_b[l].reshape(1, IDX_DIM), 1, LANES)

    def front(x, pos):
        m = x.shape[0]
        xn = _rmsnorm(x, norm_mix_g[l], min(512, m))
        proj = _matmul(xn, w_p, min(512, m), 1024)
        return proj, _prep(proj, _rope_tables(pos), qg, kg, lg, lb)

    xp = x_prompt.reshape(seq, D_MODEL)
    proj_p, pp_ = front(xp, jnp.arange(seq))
    attn_p = _attn_prompt(pp_, min(TOPK_MAX, seq // 4))
    ssm_p, state_p = _ssd_prompt(proj_p, sc)
    y_p = _trunk(xp, proj_p, attn_p, ssm_p, w)

    xs = x_sample.reshape(db * t_new, D_MODEL)
    proj_s, ps_ = front(xs, jnp.tile(past + jnp.arange(t_new), db))
    rh = Q_PER_KV
    q_rows = _pad_rows(ps_["q_bf"].reshape(db, t_new, KV_HEADS, rh, HEAD_DIM).transpose(0, 2, 3, 1, 4), 3, Q_SLOTS)
    q_rows = q_rows.reshape(db, KV_HEADS, rh * Q_SLOTS, HEAD_DIM)
    iq_rows = _pad_rows(ps_["iq_bf"].reshape(db, t_new, IDX_HEADS, IDX_DIM).transpose(0, 2, 1, 3), 2, Q_SLOTS)
    iq_rows = iq_rows.reshape(db, IDX_HEADS * Q_SLOTS, IDX_DIM)
    w_rows = _pad_rows(ps_["wT"].reshape(IDX_HEADS, db, t_new).transpose(1, 0, 2), 2, Q_SLOTS)
    w_rows = jnp.broadcast_to(w_rows.reshape(db, IDX_HEADS * Q_SLOTS, 1), (db, IDX_HEADS * Q_SLOTS, LANES))
    ikn = _pad_rows(ps_["ik_bf"].reshape(db, t_new, IDX_DIM), 1, LANES)
    kn = _pad_rows(ps_["k_f32"].reshape(db, t_new, nkv), 1, LANES)
    v_s = proj_s[:, C_V:C_V + nkv]
    vn = _pad_rows(v_s.reshape(db, t_new, nkv), 1, LANES)
    bias = _scores_sample(page_table, cache_idx_k[l], iq_rows, w_rows, ikn, t_new, min(TOPK_MAX, (past + t_new) // 4))
    n_pool = cache_k.shape[1]
    o_rows = _attn_sample(page_table, cache_k[l].reshape(n_pool, page, nkv), cache_v[l].reshape(n_pool, page, nkv),
                          kn, vn, q_rows, bias)
    attn_s = o_rows.reshape(db, KV_HEADS, rh, Q_SLOTS, HEAD_DIM)[:, :, :, :t_new].transpose(0, 3, 1, 2, 4)
    attn_s = attn_s.reshape(db * t_new, N_HEADS * HEAD_DIM)
    ssm_s, state_s = _ssd_sample(proj_s.reshape(db, t_new, N_PROJ), state_conv[l], state_ssm[l], sc)
    y_s = _trunk(xs, proj_s, attn_s, ssm_s.reshape(db * t_new, D_INNER), w)

    kv_shape = lambda b, t: (1, b, t, KV_HEADS, HEAD_DIM)
    xbc_p = proj_p[seq - (CONV_W - 1):, C_XBC:C_XBC + CONV_DIM]
    xbc_s = proj_s.reshape(db, t_new, N_PROJ)[:, :, C_XBC:C_XBC + CONV_DIM]
    conv_s = jnp.concatenate([state_conv[l], xbc_s], axis=1)[:, t_new:]
    return (y_p.reshape(bp, seq, D_MODEL), y_s.reshape(db, t_new, D_MODEL),
            pp_["k_f32"].reshape(kv_shape(bp, seq)), proj_p[:, C_V:C_V + nkv].reshape(kv_shape(bp, seq)),
            pp_["ik_f32"].reshape(1, bp, seq, IDX_DIM), xbc_p.reshape(1, bp, CONV_W - 1, CONV_DIM),
            state_p.reshape(1, bp, SSM_HEADS, SSM_HEAD_DIM, D_STATE),
            ps_["k_f32"].reshape(kv_shape(db, t_new)), v_s.reshape(kv_shape(db, t_new)),
            ps_["ik_f32"].reshape(1, db, t_new, IDX_DIM), conv_s.reshape(1, db, CONV_W - 1, CONV_DIM),
            state_s.reshape(1, db, SSM_HEADS, SSM_HEAD_DIM, D_STATE))
```

```python
import functools
import math

import jax
import jax.numpy as jnp
import numpy as np
from jax import lax
from jax.experimental import pallas as pl
from jax.experimental.pallas import tpu as pltpu

F32, BF16, I32 = jnp.float32, jnp.bfloat16, jnp.int32

D_MODEL = 2048
N_HEADS = 16
KV_HEADS = 4
HEAD_DIM = 128
Q_PER_KV = N_HEADS // KV_HEADS
IDX_HEADS = 16
IDX_DIM = 64
TOPK_MAX = 256
D_INNER = 4096
SSM_HEAD_DIM = 64
SSM_HEADS = 64
SSM_GROUPS = 8
D_STATE = 128
CONV_W = 4
CONV_DIM = D_INNER + 2 * SSM_GROUPS * D_STATE
SSD_CHUNK = 128
D_FF = 4 * D_MODEL
ROPE_THETA = 10000.0
EPS = 1e-6
SPLITS = (N_HEADS * HEAD_DIM, KV_HEADS * HEAD_DIM, KV_HEADS * HEAD_DIM, IDX_HEADS * IDX_DIM, IDX_DIM, IDX_HEADS,
          D_INNER, CONV_DIM, SSM_HEADS, D_MODEL, D_MODEL)

LANES = 128
SUBLANES = 8
VMEM_LIMIT = 56 * 1024 * 1024

C_Z = 0
C_Q = 4096
C_XBC = 6144
C_GA = 12288
C_GB = 14336
C_K = 16384
C_V = 16896
C_IQ = 17408
C_MISC = 18432
MISC_W = 512
N_PROJ = 19456

INT_MIN = -(2 ** 31)
INT_MAX = 2 ** 31 - 1
NEG_INF_KEY = -2139095041
NEG_BIG = -1e30
Q_TILE = 128
KEY_CHUNK = 512
LOG2E = 1.4426950408889634


def _cparams(*sem):
    return pltpu.CompilerParams(dimension_semantics=sem, vmem_limit_bytes=VMEM_LIMIT)


def _sortable_key(x):
    bits = pltpu.bitcast(x, I32)
    return bits ^ ((bits >> 31) & INT_MAX)


def _rmsnorm_kernel(x_ref, g_ref, o_ref):
    x = x_ref[...]
    ms = jnp.mean(x * x, axis=-1, keepdims=True)
    o_ref[...] = (x * lax.rsqrt(ms + EPS) * g_ref[...]).astype(o_ref.dtype)


def _rmsnorm(x, g, tm):
    m, d = x.shape
    return pl.pallas_call(
        _rmsnorm_kernel, grid=(m // tm,),
        in_specs=[pl.BlockSpec((tm, d), lambda i: (i, 0)), pl.BlockSpec((1, d), lambda i: (0, 0))],
        out_specs=pl.BlockSpec((tm, d), lambda i: (i, 0)),
        out_shape=jax.ShapeDtypeStruct((m, d), BF16),
        compiler_params=_cparams("parallel"), name="rmsnorm",
    )(x, g.reshape(1, d))


def _matmul_kernel(a_ref, w_ref, o_ref):
    o_ref[...] = jnp.dot(a_ref[...], w_ref[...], preferred_element_type=F32)


def _matmul(a, w, tm, tn):
    m, k = a.shape
    n = w.shape[1]
    return pl.pallas_call(
        _matmul_kernel, grid=(n // tn, m // tm),
        in_specs=[pl.BlockSpec((tm, k), lambda j, i: (i, 0)), pl.BlockSpec((k, tn), lambda j, i: (0, j))],
        out_specs=pl.BlockSpec((tm, tn), lambda j, i: (i, j)),
        out_shape=jax.ShapeDtypeStruct((m, n), F32),
        compiler_params=_cparams("parallel", "parallel"), name="in_proj",
    )(a, w)


def _mix_kernel(a_ref, s_ref, wa_ref, ws_ref, ga_ref, gb_ref, o_ref):
    pa = jnp.dot(a_ref[...], wa_ref[...], preferred_element_type=F32)
    ps = jnp.dot(s_ref[...], ws_ref[...], preferred_element_type=F32)
    o_ref[...] = (jax.nn.sigmoid(ga_ref[...]) * pa + jax.nn.sigmoid(gb_ref[...]) * ps).astype(o_ref.dtype)


def _mix(attn_o, ssm_o, wa, ws, proj, tm, tn):
    m = attn_o.shape[0]
    return pl.pallas_call(
        _mix_kernel, grid=(D_MODEL // tn, m // tm),
        in_specs=[pl.BlockSpec((tm, N_HEADS * HEAD_DIM), lambda j, i: (i, 0)),
                  pl.BlockSpec((tm, D_INNER), lambda j, i: (i, 0)),
                  pl.BlockSpec((N_HEADS * HEAD_DIM, tn), lambda j, i: (0, j)),
                  pl.BlockSpec((D_INNER, tn), lambda j, i: (0, j)),
                  pl.BlockSpec((tm, tn), lambda j, i: (i, C_GA // tn + j)),
                  pl.BlockSpec((tm, tn), lambda j, i: (i, C_GB // tn + j))],
        out_specs=pl.BlockSpec((tm, tn), lambda j, i: (i, j)),
        out_shape=jax.ShapeDtypeStruct((m, D_MODEL), BF16),
        compiler_params=_cparams("parallel", "parallel"), name="gated_merge",
    )(attn_o, ssm_o, wa, ws, proj, proj)


def _outproj_kernel(x_ref, mix_ref, w_ref, g_ref, h_ref, hn_ref):
    h = x_ref[...] + jnp.dot(mix_ref[...], w_ref[...], preferred_element_type=F32)
    h_ref[...] = h
    ms = jnp.mean(h * h, axis=-1, keepdims=True)
    hn_ref[...] = (h * lax.rsqrt(ms + EPS) * g_ref[...]).astype(hn_ref.dtype)


def _outproj(x, mixed, w, g, tm):
    m = x.shape[0]
    return pl.pallas_call(
        _outproj_kernel, grid=(m // tm,),
        in_specs=[pl.BlockSpec((tm, D_MODEL), lambda i: (i, 0)), pl.BlockSpec((tm, D_MODEL), lambda i: (i, 0)),
                  pl.BlockSpec((D_MODEL, D_MODEL), lambda i: (0, 0)), pl.BlockSpec((1, D_MODEL), lambda i: (0, 0))],
        out_specs=[pl.BlockSpec((tm, D_MODEL), lambda i: (i, 0)), pl.BlockSpec((tm, D_MODEL), lambda i: (i, 0))],
        out_shape=[jax.ShapeDtypeStruct((m, D_MODEL), F32), jax.ShapeDtypeStruct((m, D_MODEL), BF16)],
        compiler_params=_cparams("parallel"), name="out_proj",
    )(x, mixed, w, g.reshape(1, D_MODEL))


def _mlp_kernel(hn_ref, h_ref, wu_ref, wd_ref, o_ref):
    f = pl.program_id(1)
    u = jnp.dot(hn_ref[...], wu_ref[...], preferred_element_type=F32)
    act = jnp.square(jnp.maximum(u, 0.0)).astype(BF16)
    contrib = jnp.dot(act, wd_ref[...], preferred_element_type=F32)

    @pl.when(f == 0)
    def _():
        o_ref[...] = h_ref[...] + contrib

    @pl.when(f > 0)
    def _():
        o_ref[...] += contrib


def _mlp(hn, h, wu, wd, tm, tf):
    m = h.shape[0]
    return pl.pallas_call(
        _mlp_kernel, grid=(m // tm, D_FF // tf),
        in_specs=[pl.BlockSpec((tm, D_MODEL), lambda i, f: (i, 0)), pl.BlockSpec((tm, D_MODEL), lambda i, f: (i, 0)),
                  pl.BlockSpec((D_MODEL, tf), lambda i, f: (0, f)), pl.BlockSpec((tf, D_MODEL), lambda i, f: (f, 0))],
        out_specs=pl.BlockSpec((tm, D_MODEL), lambda i, f: (i, 0)),
        out_shape=jax.ShapeDtypeStruct((m, D_MODEL), F32),
        compiler_params=_cparams("parallel", "arbitrary"), name="mlp",
    )(hn, h, wu, wd)


def _prep_kernel(q_ref, k_ref, v_ref, iq_ref, misc_ref, ca_ref, sa_ref, cb_ref, sb_ref, qg_ref, kg_ref, lg_ref,
                 lb_ref, kf_ref, kb_ref, ikf_ref, ikb_ref, qn_ref, iqn_ref, qt_ref, iqt_ref, wt_ref, vt_ref):
    ca, sa, cb, sb = ca_ref[...], sa_ref[...], cb_ref[...], sb_ref[...]
    lane = lax.broadcasted_iota(I32, (Q_TILE, LANES), 1)
    first_half = (lane % IDX_DIM) < (IDX_DIM // 2)

    def norm_rope(x, g):
        ms = jnp.mean(x * x, axis=-1, keepdims=True)
        xn = x * lax.rsqrt(ms + EPS) * g
        return xn * ca + pltpu.roll(xn, HEAD_DIM // 2, 1) * sa

    def rope_idx(x):
        rot = jnp.where(first_half, pltpu.roll(x, LANES - IDX_DIM // 2, 1), pltpu.roll(x, IDX_DIM // 2, 1))
        return x * cb + rot * sb

    for h in range(N_HEADS):
        sl = slice(h * HEAD_DIM, (h + 1) * HEAD_DIM)
        qh = norm_rope(q_ref[:, sl], qg_ref[...])
        qn_ref[:, sl] = qh.astype(BF16)
        qt_ref[sl, :] = qh.T.astype(BF16)
    for h in range(KV_HEADS):
        sl = slice(h * HEAD_DIM, (h + 1) * HEAD_DIM)
        kh = norm_rope(k_ref[:, sl], kg_ref[...])
        kf_ref[:, sl] = kh
        kb_ref[:, sl] = kh.astype(BF16)
        vt_ref[sl, :] = v_ref[:, sl].T.astype(BF16)
    for c in range(IDX_HEADS * IDX_DIM // LANES):
        sl = slice(c * LANES, (c + 1) * LANES)
        x = rope_idx(iq_ref[:, sl])
        iqn_ref[:, sl] = x.astype(BF16)
        iqt_ref[sl, :] = x.T.astype(BF16)

    m = misc_ref[:, 0:LANES]
    is_k = lane < IDX_DIM
    mu = jnp.sum(jnp.where(is_k, m, 0.0), axis=-1, keepdims=True) * (1.0 / IDX_DIM)
    xc = jnp.where(is_k, m - mu, 0.0)
    var = jnp.sum(xc * xc, axis=-1, keepdims=True) * (1.0 / IDX_DIM)
    y = rope_idx(xc * lax.rsqrt(var + EPS) * lg_ref[...] + lb_ref[...])
    ikf_ref[...] = y[:, :IDX_DIM]
    ikb_ref[...] = y[:, :IDX_DIM].astype(BF16)
    wt_ref[...] = m.T[IDX_DIM:IDX_DIM + IDX_HEADS, :] * (IDX_HEADS ** -0.5 * IDX_DIM ** -0.5)


def _prep(proj, tabs, qg, kg, lg, lb):
    t = proj.shape[0]
    tq = Q_TILE
    row = lambda w, cb: pl.BlockSpec((tq, w), lambda i, cb=cb: (i, cb))
    const = lambda w: pl.BlockSpec((1, w), lambda i: (0, 0))
    colT = lambda r: pl.BlockSpec((r, tq), lambda i: (0, i))
    nq, nkv, niq = N_HEADS * HEAD_DIM, KV_HEADS * HEAD_DIM, IDX_HEADS * IDX_DIM
    outs = pl.pallas_call(
        _prep_kernel, grid=(t // tq,),
        in_specs=[row(nq, C_Q // nq), row(nkv, C_K // nkv), row(nkv, C_V // nkv), row(niq, C_IQ // niq),
                  row(MISC_W, C_MISC // MISC_W), row(LANES, 0), row(LANES, 0), row(LANES, 0), row(LANES, 0),
                  const(LANES), const(LANES), const(LANES), const(LANES)],
        out_specs=[row(nkv, 0), row(nkv, 0), row(IDX_DIM, 0), row(IDX_DIM, 0), row(nq, 0), row(niq, 0),
                   colT(nq), colT(niq), colT(IDX_HEADS), colT(nkv)],
        out_shape=[jax.ShapeDtypeStruct((t, nkv), F32), jax.ShapeDtypeStruct((t, nkv), BF16),
                   jax.ShapeDtypeStruct((t, IDX_DIM), F32), jax.ShapeDtypeStruct((t, IDX_DIM), BF16),
                   jax.ShapeDtypeStruct((t, nq), BF16), jax.ShapeDtypeStruct((t, niq), BF16),
                   jax.ShapeDtypeStruct((nq, t), BF16), jax.ShapeDtypeStruct((niq, t), BF16),
                   jax.ShapeDtypeStruct((IDX_HEADS, t), F32), jax.ShapeDtypeStruct((nkv, t), BF16)],
        compiler_params=_cparams("parallel"), name="norm_rope_prep",
    )(proj, proj, proj, proj, proj, *tabs, qg, kg, lg, lb)
    return dict(zip(("k_f32", "k_bf", "ik_f32", "ik_bf", "q_bf", "iq_bf", "qT", "iqT", "wT", "vT"), outs))


def _rope_tables(pos):
    def tab(dim):
        half = dim // 2
        inv = ROPE_THETA ** (-jnp.arange(half, dtype=F32) / half)
        ang = pos.astype(F32)[:, None] * inv[None, :]
        c, s = jnp.cos(ang), jnp.sin(ang)
        rep = LANES // dim
        return jnp.tile(jnp.concatenate([c, c], -1), (1, rep)), jnp.tile(jnp.concatenate([-s, s], -1), (1, rep))
    ca, sa = tab(HEAD_DIM)
    cb, sb = tab(IDX_DIM)
    return ca, sa, cb, sb


def _select_threshold(count_ge, count_gt, count_eq_before, topk, idx_bits, live):
    tau = jnp.full(live.shape, INT_MIN, I32)
    cand = tau ^ INT_MIN
    tau = jnp.where(count_ge(cand) >= topk, cand, tau)

    def bit_step(b, tau):
        cand = tau | lax.shift_left(jnp.int32(1), 30 - b)
        return jnp.where(count_ge(cand) >= topk, cand, tau)

    tau = lax.fori_loop(0, 31, bit_step, tau)
    n_gt = count_gt(tau)
    need = topk - n_gt
    n_eq = count_ge(tau) - n_gt
    tie = jnp.logical_and(jnp.logical_and(n_eq > need, tau != NEG_INF_KEY), live)
    has_tie = jnp.max(tie.astype(I32)) > 0

    def with_ties():
        def step(b, cut):
            cand = cut | lax.shift_left(jnp.int32(1), idx_bits - 1 - b)
            return jnp.where(count_eq_before(tau, cand) < need, cand, cut)
        return lax.fori_loop(0, idx_bits, step, jnp.zeros(live.shape, I32))

    cut = lax.cond(has_tie, with_ties, lambda: jnp.full(live.shape, INT_MAX, I32))
    return tau, cut


def _attn_prompt_kernel(bound_ref, ik_ref, k_ref, vt_ref, iqt_ref, wt_ref, qt_ref, o_ref, keys_ref, acc_ref, *, topk,
                        idx_bits, bounded):
    i = pl.program_id(0)
    ch = KEY_CHUNK
    n_ch = (i * Q_TILE + Q_TILE + ch - 1) // ch
    t_glob = i * Q_TILE + lax.broadcasted_iota(I32, (1, Q_TILE), 1)
    row_iota = lax.broadcasted_iota(I32, (ch, Q_TILE), 0)

    def score_chunk(c, carry):
        base = pl.multiple_of(c * ch, ch)
        ikc = ik_ref[pl.ds(base, ch), :]
        acc = jnp.zeros((ch, Q_TILE), F32)
        for h in range(0, IDX_HEADS, 2):
            pair = jnp.concatenate([iqt_ref[h * IDX_DIM:(h + 1) * IDX_DIM, :],
                                    iqt_ref[(h + 1) * IDX_DIM:(h + 2) * IDX_DIM, :]], axis=1)
            d = jnp.dot(ikc, pair, preferred_element_type=F32)
            acc = acc + jnp.maximum(d[:, :Q_TILE], 0.0) * wt_ref[h:h + 1, :]
            acc = acc + jnp.maximum(d[:, Q_TILE:], 0.0) * wt_ref[h + 1:h + 2, :]
        acc = jnp.where(base + row_iota <= t_glob, acc, -jnp.inf)
        keys_ref[pl.ds(base, ch), :] = _sortable_key(acc)
        return carry

    lax.fori_loop(0, n_ch, score_chunk, 0)

    def _count(pred):
        def body(c, cnt):
            base = pl.multiple_of(c * ch, ch)
            blk = keys_ref[pl.ds(base, ch), :]
            hit = pred(blk, base).astype(I32).reshape(ch // SUBLANES, SUBLANES, Q_TILE)
            return cnt + jnp.sum(hit, axis=0)
        cnt = lax.fori_loop(0, n_ch, body, jnp.zeros((SUBLANES, Q_TILE), I32))
        return jnp.sum(cnt, axis=0, keepdims=True)

    count_ge = lambda cand: _count(lambda blk, base: blk >= cand)
    count_gt = lambda cand: _count(lambda blk, base: blk > cand)
    count_eq_before = lambda tau, cut: _count(
        lambda blk, base: jnp.where(blk == tau, (base + row_iota < cut).astype(I32), 0))
    tau, cut = _select_threshold(count_ge, count_gt, count_eq_before, topk, idx_bits,
                                 jnp.ones((1, Q_TILE), jnp.bool_))

    offset = -bound_ref[0] if bounded else 0.0

    def bias_chunk(c, carry):
        base = pl.multiple_of(c * ch, ch)
        blk = keys_ref[pl.ds(base, ch), :]
        idx = base + row_iota
        sel = jnp.where(blk == tau, (idx <= cut).astype(I32), (blk > tau).astype(I32))
        sel = jnp.where(idx <= t_glob, sel, 0)
        keys_ref[pl.ds(base, ch), :] = pltpu.bitcast(jnp.where(sel > 0, offset, NEG_BIG).astype(F32), I32)
        return carry

    lax.fori_loop(0, n_ch, bias_chunk, 0)

    scale2 = HEAD_DIM ** -0.5 * LOG2E
    qgs = [jnp.concatenate([qt_ref[(g * Q_PER_KV + r) * HEAD_DIM:(g * Q_PER_KV + r + 1) * HEAD_DIM, :]
                            for r in range(Q_PER_KV)], axis=1) for g in range(KV_HEADS)]
    acc_ref[...] = jnp.zeros_like(acc_ref)

    def attn_chunk(c, carry):
        base = pl.multiple_of(c * ch, ch)
        bias = pltpu.bitcast(keys_ref[pl.ds(base, ch), :], F32)
        bias = jnp.concatenate([bias] * Q_PER_KV, axis=1)
        out = []
        for g in range(KV_HEADS):
            kc = k_ref[pl.ds(base, ch), g * HEAD_DIM:(g + 1) * HEAD_DIM]
            v_t = vt_ref[g * HEAD_DIM:(g + 1) * HEAD_DIM, pl.ds(base, ch)]
            s = jnp.dot(kc, qgs[g], preferred_element_type=F32) + bias
            if bounded:
                p = jnp.exp2(s * scale2)
                acc_ref[g] += jnp.dot(v_t, p.astype(BF16), preferred_element_type=F32)
                out.append(carry[g] + jnp.sum(p, axis=0, keepdims=True))
            else:
                m, l = carry[g]
                m_new = jnp.maximum(m, jnp.max(s, axis=0, keepdims=True))
                p = jnp.exp2((s - m_new) * scale2)
                alpha = jnp.exp2((m - m_new) * scale2)
                acc_ref[g] = acc_ref[g] * alpha + jnp.dot(v_t, p.astype(BF16), preferred_element_type=F32)
                out.append((m_new, l * alpha + jnp.sum(p, axis=0, keepdims=True)))
        return tuple(out)

    l0 = jnp.zeros((1, Q_PER_KV * Q_TILE), F32)
    init = l0 if bounded else (jnp.full((1, Q_PER_KV * Q_TILE), NEG_BIG, F32), l0)
    fin = lax.fori_loop(0, n_ch, attn_chunk, (init,) * KV_HEADS)
    for g in range(KV_HEADS):
        o = acc_ref[g] / (fin[g] if bounded else fin[g][1])
        for r in range(Q_PER_KV):
            h = g * Q_PER_KV + r
            o_ref[:, h * HEAD_DIM:(h + 1) * HEAD_DIM] = o[:, r * Q_TILE:(r + 1) * Q_TILE].T.astype(o_ref.dtype)


MAX_BOUNDED_LOGIT = 40.0


def _attn_prompt_call(p, topk, bound, bounded):
    t = p["k_bf"].shape[0]
    assert t % KEY_CHUNK == 0, "prompt length must be a multiple of the key chunk"
    nq, nkv, niq = N_HEADS * HEAD_DIM, KV_HEADS * HEAD_DIM, IDX_HEADS * IDX_DIM
    full = lambda shape: pl.BlockSpec(shape, lambda i: (0, 0))
    colT = lambda r: pl.BlockSpec((r, Q_TILE), lambda i: (0, i))
    kern = functools.partial(_attn_prompt_kernel, topk=topk, idx_bits=max(1, (t - 1).bit_length()), bounded=bounded)
    return pl.pallas_call(
        kern, grid=(t // Q_TILE,),
        in_specs=[pl.BlockSpec(memory_space=pltpu.SMEM), full((t, IDX_DIM)), full((t, nkv)), full((nkv, t)),
                  colT(niq), colT(IDX_HEADS), colT(nq)],
        out_specs=pl.BlockSpec((Q_TILE, nq), lambda i: (i, 0)),
        out_shape=jax.ShapeDtypeStruct((t, nq), BF16),
        scratch_shapes=[pltpu.VMEM((t, Q_TILE), I32), pltpu.VMEM((KV_HEADS, HEAD_DIM, Q_PER_KV * Q_TILE), F32)],
        compiler_params=_cparams("arbitrary"), name="prompt_attention_bounded" if bounded else "prompt_attention",
    )(bound, p["ik_bf"], p["k_bf"], p["vT"], p["iqT"], p["wT"], p["qT"])


def _attn_prompt(p, topk, q_gain, k_gain):
    bound = (HEAD_DIM * 1.02) * jnp.max(jnp.abs(q_gain)) * jnp.max(jnp.abs(k_gain))
    bound = bound.reshape(1).astype(F32)
    return lax.cond(bound[0] * HEAD_DIM ** -0.5 <= MAX_BOUNDED_LOGIT,
                    lambda: _attn_prompt_call(p, topk, bound, True),
                    lambda: _attn_prompt_call(p, topk, bound, False))


def _softplus(x):
    return jnp.maximum(x, 0.0) + jnp.log1p(jnp.exp(-jnp.abs(x)))


def _split3(v):
    v1 = v.astype(BF16)
    r1 = v - v1.astype(F32)
    v2 = r1.astype(BF16)
    v3 = (r1 - v2.astype(F32)).astype(BF16)
    return v1, v2, v3


def _dot3(parts, e):
    return sum(jnp.dot(p, e, preferred_element_type=F32) for p in parts)


LANE_BLK = 512


def _conv_silu(ext_ref, row0, rows, cw_ref, cb_ref, xa_ref):
    for j in range(CONV_DIM // LANE_BLK):
        sl = slice(j * LANE_BLK, (j + 1) * LANE_BLK)
        acc = ext_ref[row0:row0 + rows, sl] * cw_ref[0:1, sl]
        for i in range(1, CONV_W):
            acc = acc + ext_ref[row0 + i:row0 + i + rows, sl] * cw_ref[i:i + 1, sl]
        acc = acc + cb_ref[:, sl]
        xa_ref[:, sl] = acc * jax.nn.sigmoid(acc)


def _gated_norm(y_ref, xa_ref, z_ref, dsk_ref, ng_ref, o_ref, zidx):
    for j in range(SSM_GROUPS):
        sl = slice(j * LANE_BLK, (j + 1) * LANE_BLK)
        y = y_ref[:, sl] + dsk_ref[:, sl] * xa_ref[:, sl]
        zz = z_ref[zidx + (slice(None), sl)] if zidx else z_ref[:, sl]
        y = y * (zz * jax.nn.sigmoid(zz))
        ms = jnp.mean(y * y, axis=-1, keepdims=True)
        res = (y * lax.rsqrt(ms + EPS) * ng_ref[:, sl]).astype(o_ref.dtype)
        if zidx:
            o_ref[zidx + (slice(None), sl)] = res
        else:
            o_ref[:, sl] = res


def _ssd_prompt_kernel(xbc_ref, z_ref, misc_ref, cw_ref, cb_ref, dtb_ref, alog_ref, dsk_ref, ng_ref, exp_ref, tri_ref,
                       o_ref, st_ref, ext_ref, xa_ref, xlo_ref, xhi_ref, xw_ref, ecs_ref, y_ref, state_ref):
    c = pl.program_id(0)
    q = SSD_CHUNK
    pad = SUBLANES

    @pl.when(c == 0)
    def _():
        ext_ref[0:pad, :] = jnp.zeros((pad, CONV_DIM), F32)
        state_ref[...] = jnp.zeros_like(state_ref)

    ext_ref[pad:pad + q, :] = xbc_ref[...]
    _conv_silu(ext_ref, pad - (CONV_W - 1), q, cw_ref, cb_ref, xa_ref)
    ext_ref[0:pad, :] = ext_ref[q:q + pad, :]

    dt = _softplus(misc_ref[:, LANES:2 * LANES] + dtb_ref[...])
    da = dt * (-jnp.exp(alog_ref[...]))
    cs = jnp.dot(tri_ref[...], da, precision=lax.Precision.HIGHEST, preferred_element_type=F32)
    cs_t = cs.T
    ecs = jnp.exp(cs)
    dec_end = jnp.exp(cs[q - 1:q, :] - cs)
    parts = _split3(jnp.concatenate([dt, dt * dec_end, ecs], axis=0))
    lane = lax.broadcasted_iota(I32, (q, LANE_BLK), 1)
    lo = (lane % LANES) < SSM_HEAD_DIM
    for j in range(D_INNER // LANE_BLK):
        sl = slice(j * LANE_BLK, (j + 1) * LANE_BLK)
        ex = _dot3(parts, exp_ref[:, sl])
        xs = xa_ref[:, sl]
        xdt = xs * ex[0:q]
        xlo_ref[:, sl] = jnp.where(lo, xdt, 0.0).astype(BF16)
        xhi_ref[:, sl] = jnp.where(lo, 0.0, xdt).astype(BF16)
        xw_ref[:, sl] = (xs * ex[q:2 * q]).astype(BF16)
        ecs_ref[:, sl] = ex[2 * q:3 * q]

    causal = lax.broadcasted_iota(I32, (q, q), 0) >= lax.broadcasted_iota(I32, (q, q), 1)
    heads_per_group = SSM_HEADS // SSM_GROUPS
    for g in range(SSM_GROUPS):
        b0 = D_INNER + g * D_STATE
        c0 = D_INNER + SSM_GROUPS * D_STATE + g * D_STATE
        bg = xa_ref[:, b0:b0 + D_STATE]
        cg = xa_ref[:, c0:c0 + D_STATE].astype(BF16)
        cbm = lax.dot_general(cg, bg.astype(BF16), (((1,), (1,)), ((), ())), preferred_element_type=F32)
        bg_t = bg.T.astype(BF16)
        for j in range(heads_per_group // 2):
            pidx = g * (heads_per_group // 2) + j
            sl = slice(pidx * LANES, (pidx + 1) * LANES)
            yd = None
            for hh, x_ref in ((0, xlo_ref), (1, xhi_ref)):
                h = 2 * pidx + hh
                lm = jnp.exp(jnp.where(causal, cs[:, h:h + 1] - cs_t[h:h + 1, :], -jnp.inf))
                t = jnp.dot((cbm * lm).astype(BF16), x_ref[:, sl], preferred_element_type=F32)
                yd = t if yd is None else yd + t
            st = state_ref[pidx]
            yo = jnp.dot(cg, st.astype(BF16), preferred_element_type=F32) * ecs_ref[:, sl]
            y_ref[:, sl] = yd + yo
            state_ref[pidx] = st * ecs_ref[q - 1:q, sl] + jnp.dot(bg_t, xw_ref[:, sl], preferred_element_type=F32)

    _gated_norm(y_ref, xa_ref, z_ref, dsk_ref, ng_ref, o_ref, ())

    @pl.when(c == pl.num_programs(0) - 1)
    def _():
        for p in range(SSM_HEADS // 2):
            st_ref[2 * p:2 * p + 2] = state_ref[p].T.reshape(2, SSM_HEAD_DIM, D_STATE)


def _ssm_consts(conv_w, conv_b, dt_bias, a_log, d_skip, ssm_norm_g):
    pad_h = lambda v: jnp.pad(v.reshape(1, SSM_HEADS), ((0, 0), (0, LANES - SSM_HEADS)))
    head_of_lane = np.arange(D_INNER) // SSM_HEAD_DIM
    expand = jnp.asarray((np.arange(LANES)[:, None] == head_of_lane[None, :]), BF16)
    return dict(cw=conv_w, cb=conv_b.reshape(1, CONV_DIM), dtb=pad_h(dt_bias), alog=pad_h(a_log),
                dsk=jnp.repeat(d_skip, SSM_HEAD_DIM).reshape(1, D_INNER), ng=ssm_norm_g.reshape(1, D_INNER),
                expand=expand)


def _ssd_prompt(proj, sc):
    t = proj.shape[0]
    q = SSD_CHUNK
    assert t % q == 0
    tri = jnp.asarray(np.tril(np.ones((q, q), np.float32)))
    row = lambda w, cb: pl.BlockSpec((q, w), lambda i, cb=cb: (i, cb))
    const = lambda r, w: pl.BlockSpec((r, w), lambda i: (0, 0))
    return pl.pallas_call(
        _ssd_prompt_kernel, grid=(t // q,),
        in_specs=[row(CONV_DIM, C_XBC // CONV_DIM), row(D_INNER, C_Z // D_INNER), row(MISC_W, C_MISC // MISC_W),
                  const(CONV_W, CONV_DIM), const(1, CONV_DIM), const(1, LANES), const(1, LANES),
                  const(1, D_INNER), const(1, D_INNER), const(LANES, D_INNER), const(q, q)],
        out_specs=[pl.BlockSpec((q, D_INNER), lambda i: (i, 0)),
                   pl.BlockSpec((SSM_HEADS, SSM_HEAD_DIM, D_STATE), lambda i: (0, 0, 0))],
        out_shape=[jax.ShapeDtypeStruct((t, D_INNER), BF16),
                   jax.ShapeDtypeStruct((SSM_HEADS, SSM_HEAD_DIM, D_STATE), F32)],
        scratch_shapes=[pltpu.VMEM((q + 2 * SUBLANES, CONV_DIM), F32), pltpu.VMEM((q, CONV_DIM), F32),
                        pltpu.VMEM((q, D_INNER), BF16), pltpu.VMEM((q, D_INNER), BF16), pltpu.VMEM((q, D_INNER), BF16),
                        pltpu.VMEM((q, D_INNER), F32), pltpu.VMEM((q, D_INNER), F32),
                        pltpu.VMEM((SSM_HEADS // 2, D_STATE, LANES), F32)],
        compiler_params=_cparams("arbitrary"), name="ssd_prompt",
    )(proj, proj, proj, sc["cw"], sc["cb"], sc["dtb"], sc["alog"], sc["dsk"], sc["ng"], sc["expand"], tri)


def _ssd_sample_kernel(xbc_ref, z_ref, misc_ref, cst_ref, st_ref, cw_ref, cb_ref, dtb_ref, alog_ref, dsk_ref, ng_ref,
                       exp_ref, gsum_ref, gexp_ref, o_ref, nst_ref, ext_ref, xa_ref, y_ref, *, t):
    nt = (((1,), (1,)), ((), ()))
    tn = (((0,), (0,)), ((), ()))
    ext_ref[...] = jnp.zeros_like(ext_ref)
    ext_ref[0:CONV_W - 1, :] = cst_ref[0]
    ext_ref[CONV_W - 1:CONV_W - 1 + t, :] = xbc_ref[0]
    _conv_silu(ext_ref, 0, t, cw_ref, cb_ref, xa_ref)

    dt = _softplus(misc_ref[0][:, LANES:2 * LANES] + dtb_ref[...])
    da = dt * (-jnp.exp(alog_ref[...]))
    rows = [da[0:1]]
    for i in range(1, t):
        rows.append(rows[-1] + da[i:i + 1])
    cs = jnp.concatenate(rows, axis=0)
    ecs = jnp.exp(cs)
    dec_end = jnp.exp(cs[t - 1:t] - cs)
    parts = _split3(jnp.concatenate([dt, dt * dec_end, ecs, cs], axis=0))

    pairs = [(l, s) for l in range(t) for s in range(l + 1)]
    n_b = SSM_GROUPS * D_STATE
    bm = xa_ref[:, D_INNER:D_INNER + n_b]
    cm = xa_ref[:, D_INNER + n_b:D_INNER + 2 * n_b]
    prod = jnp.concatenate([cm[l:l + 1] * bm[s:s + 1] for l, s in pairs], axis=0)
    cb_parts = _split3(_dot3(_split3(prod), gsum_ref[...]))

    heads_per_group = SSM_HEADS // SSM_GROUPS
    for g in range(SSM_GROUPS):
        sl = slice(g * LANE_BLK, (g + 1) * LANE_BLK)
        ex = _dot3(parts, exp_ref[:, sl])
        cbe = _dot3(cb_parts, gexp_ref[:, sl])
        xs = xa_ref[:, sl]
        xdt, xw, ecs_e, cs_e = xs * ex[0:t], xs * ex[t:2 * t], ex[2 * t:3 * t], ex[3 * t:4 * t]
        yrows = []
        for l in range(t):
            acc = None
            for s in range(l + 1):
                pi = pairs.index((l, s))
                coef = cbe[pi:pi + 1] if s == l else cbe[pi:pi + 1] * jnp.exp(cs_e[l:l + 1] - cs_e[s:s + 1])
                term = coef * xdt[s:s + 1]
                acc = term if acc is None else acc + term
            yrows.append(acc)
        sg = st_ref[0, g * heads_per_group:(g + 1) * heads_per_group].reshape(LANE_BLK, D_STATE)
        cg = xa_ref[:, D_INNER + n_b + g * D_STATE:D_INNER + n_b + (g + 1) * D_STATE].astype(BF16)
        yo = lax.dot_general(cg, sg.astype(BF16), nt, preferred_element_type=F32) * ecs_e
        y_ref[:, sl] = jnp.concatenate(yrows, axis=0) + yo
        bg = xa_ref[:, D_INNER + g * D_STATE:D_INNER + (g + 1) * D_STATE]
        zpad = jnp.zeros((SUBLANES - t, LANE_BLK), F32)
        upd = lax.dot_general(jnp.concatenate([xw, zpad], axis=0).astype(BF16),
                              jnp.concatenate([bg, zpad[:, :D_STATE]], axis=0).astype(BF16), tn,
                              preferred_element_type=F32)
        for r in range(heads_per_group):
            h = g * heads_per_group + r
            nst_ref[0, h] = st_ref[0, h] * ecs[t - 1:t, h:h + 1] + upd[r * SSM_HEAD_DIM:(r + 1) * SSM_HEAD_DIM]

    _gated_norm(y_ref, xa_ref, z_ref, dsk_ref, ng_ref, o_ref, (0,))


def _ssd_sample(proj3, conv_state, ssm_state, sc):
    b, t = proj3.shape[:2]
    assert t + CONV_W - 1 <= SUBLANES
    n_pairs = t * (t + 1) // 2
    n_b = SSM_GROUPS * D_STATE
    gsum = jnp.asarray(np.arange(n_b)[:, None] // D_STATE == np.arange(LANES)[None, :], BF16)
    gexp = jnp.asarray(np.arange(LANES)[:, None] == (np.arange(D_INNER) // LANE_BLK)[None, :], BF16)
    row = lambda w, cb: pl.BlockSpec((1, t, w), lambda i, cb=cb: (i, 0, cb))
    const = lambda r, w: pl.BlockSpec((r, w), lambda i: (0, 0))
    st_spec = pl.BlockSpec((1, SSM_HEADS, SSM_HEAD_DIM, D_STATE), lambda i: (i, 0, 0, 0))
    kern = functools.partial(_ssd_sample_kernel, t=t)
    del n_pairs
    return pl.pallas_call(
        kern, grid=(b,),
        in_specs=[row(CONV_DIM, C_XBC // CONV_DIM), row(D_INNER, C_Z // D_INNER), row(MISC_W, C_MISC // MISC_W),
                  pl.BlockSpec((1, CONV_W - 1, CONV_DIM), lambda i: (i, 0, 0)), st_spec,
                  const(CONV_W, CONV_DIM), const(1, CONV_DIM), const(1, LANES), const(1, LANES),
                  const(1, D_INNER), const(1, D_INNER), const(LANES, D_INNER), const(n_b, LANES),
                  const(LANES, D_INNER)],
        out_specs=[pl.BlockSpec((1, t, D_INNER), lambda i: (i, 0, 0)), st_spec],
        out_shape=[jax.ShapeDtypeStruct((b, t, D_INNER), BF16),
                   jax.ShapeDtypeStruct((b, SSM_HEADS, SSM_HEAD_DIM, D_STATE), F32)],
        scratch_shapes=[pltpu.VMEM((SUBLANES, CONV_DIM), F32), pltpu.VMEM((t, CONV_DIM), F32),
                        pltpu.VMEM((t, D_INNER), F32)],
        compiler_params=_cparams("parallel"), name="ssd_sample",
    )(proj3, proj3, proj3, conv_state, ssm_state, sc["cw"], sc["cb"], sc["dtb"], sc["alog"], sc["dsk"], sc["ng"],
      sc["expand"], gsum, gexp)


SEQS_PER_STEP = 8
Q_SLOTS = SUBLANES
NT_DIMS = (((1,), (1,)), ((), ()))


def _scores_sample_kernel(pt_ref, *refs, past, t_new, topk, idx_bits):
    del pt_ref
    gs = SEQS_PER_STEP
    page_refs = refs[:gs]
    iq_ref, w_ref, ikn_ref, bias_ref, keys_ref = refs[gs:]
    j = pl.program_id(1)
    lp = past + LANES

    def head_sum(d, w):
        s = jnp.maximum(d, 0.0) * w
        return jnp.sum(s.reshape(IDX_HEADS, Q_SLOTS, LANES), axis=0)

    for k in range(gs):
        d = jnp.dot(iq_ref[k], page_refs[k][0].astype(BF16), preferred_element_type=F32)
        keys_ref[k, :, pl.ds(pl.multiple_of(j * LANES, LANES), LANES)] = _sortable_key(head_sum(d, w_ref[k]))

    @pl.when(j == pl.num_programs(1) - 1)
    def _():
        tq = lax.broadcasted_iota(I32, (Q_SLOTS, LANES), 0)
        jn = lax.broadcasted_iota(I32, (Q_SLOTS, LANES), 1)
        ok_new = jnp.logical_and(jn <= tq, jn < t_new)
        for k in range(gs):
            d = lax.dot_general(iq_ref[k], ikn_ref[k], NT_DIMS, preferred_element_type=F32)
            s = head_sum(d, w_ref[k])
            keys_ref[k, :, past:lp] = _sortable_key(jnp.where(ok_new, s, -jnp.inf))

        idx = lax.broadcasted_iota(I32, (gs, Q_SLOTS, lp), 2)
        slot = lax.broadcasted_iota(I32, (gs, Q_SLOTS, lp), 1)
        cnt = lambda hit: jnp.sum(hit.astype(I32), axis=2, keepdims=True)
        count_ge = lambda cand: cnt(keys_ref[...] >= cand)
        count_gt = lambda cand: cnt(keys_ref[...] > cand)
        count_eq_before = lambda tau, cutc: cnt(jnp.where(keys_ref[...] == tau, (idx < cutc).astype(I32), 0))
        live = lax.broadcasted_iota(I32, (gs, Q_SLOTS, 1), 1) < t_new
        tau, cut = _select_threshold(count_ge, count_gt, count_eq_before, topk, idx_bits, live)
        keys = keys_ref[...]
        sel = jnp.where(keys == tau, (idx <= cut).astype(I32), (keys > tau).astype(I32))
        new_ok = jnp.where(idx - past <= slot, (idx - past < t_new).astype(I32), 0)
        valid = jnp.where(idx < past, 1, new_ok)
        bias_ref[...] = jnp.where(sel * valid > 0, 0.0, NEG_BIG).astype(F32)


def _scores_sample(page_table, cache_idx_t, iq_rows, w_rows, ikn, t_new, topk):
    b, n_pages = page_table.shape
    page = cache_idx_t.shape[2]
    assert page == LANES and b % SEQS_PER_STEP == 0
    past = n_pages * page
    lp = past + LANES
    gs = SEQS_PER_STEP
    page_spec = lambda k: pl.BlockSpec((1, IDX_DIM, page), lambda sg, j, pt, k=k: (pt[sg * gs + k, j], 0, 0))
    grp = lambda r, w: pl.BlockSpec((gs, r, w), lambda sg, j, pt: (sg, 0, 0))
    rows = IDX_HEADS * Q_SLOTS
    kern = functools.partial(_scores_sample_kernel, past=past, t_new=t_new, topk=topk,
                             idx_bits=max(1, (lp - 1).bit_length()))
    return pl.pallas_call(
        kern,
        grid_spec=pltpu.PrefetchScalarGridSpec(
            num_scalar_prefetch=1, grid=(b // gs, n_pages),
            in_specs=[page_spec(k) for k in range(gs)] + [grp(rows, IDX_DIM), grp(rows, LANES), grp(LANES, IDX_DIM)],
            out_specs=grp(Q_SLOTS, lp),
            scratch_shapes=[pltpu.VMEM((gs, Q_SLOTS, lp), I32)]),
        out_shape=jax.ShapeDtypeStruct((b, Q_SLOTS, lp), F32),
        compiler_params=_cparams("parallel", "arbitrary"), name="sample_index_select",
    )(page_table, *([cache_idx_t] * gs), iq_rows, w_rows, ikn)


def _attn_sample_kernel(pt_ref, *refs, n_pages):
    del pt_ref
    k_refs, v_refs = refs[:n_pages], refs[n_pages:2 * n_pages]
    kn_ref, vn_ref, q_ref, bias_ref, o_ref = refs[2 * n_pages:]
    scale2 = HEAD_DIM ** -0.5 * LOG2E
    past = n_pages * LANES
    bias_r = jnp.concatenate([bias_ref[0]] * Q_PER_KV, axis=0)
    for g in range(KV_HEADS):
        sl = slice(g * HEAD_DIM, (g + 1) * HEAD_DIM)
        head = lambda ref: ref[pl.ds(g, LANES, stride=KV_HEADS), :].astype(BF16)
        qg = q_ref[0, g]
        sc = [lax.dot_general(qg, head(r), NT_DIMS, preferred_element_type=F32) for r in k_refs]
        sc.append(lax.dot_general(qg, kn_ref[0][:, sl].astype(BF16), NT_DIMS, preferred_element_type=F32))
        s = jnp.concatenate(sc, axis=1) + bias_r
        m = jnp.max(s, axis=-1, keepdims=True)
        p = jnp.exp2((s - m) * scale2)
        l = jnp.sum(p, axis=-1, keepdims=True)
        pb = p.astype(BF16)
        acc = jnp.dot(pb[:, past:], vn_ref[0][:, sl].astype(BF16), preferred_element_type=F32)
        for j, r in enumerate(v_refs):
            acc = acc + jnp.dot(pb[:, j * LANES:(j + 1) * LANES], head(r), preferred_element_type=F32)
        o_ref[0, g] = (acc / l).astype(o_ref.dtype)


def _attn_sample(page_table, cache_k, cache_v, kn, vn, q_rows, bias):
    b, n_pages = page_table.shape
    nkv = KV_HEADS * HEAD_DIM
    rows = Q_PER_KV * Q_SLOTS
    lp = bias.shape[-1]
    page_spec = lambda k: pl.BlockSpec((LANES * KV_HEADS, HEAD_DIM), lambda i, pt, k=k: (pt[i, k], 0))
    per_seq = lambda shape: pl.BlockSpec((1,) + shape, lambda i, pt: (i,) + (0,) * len(shape))
    kern = functools.partial(_attn_sample_kernel, n_pages=n_pages)
    return pl.pallas_call(
        kern,
        grid_spec=pltpu.PrefetchScalarGridSpec(
            num_scalar_prefetch=1, grid=(b,),
            in_specs=[page_spec(k) for k in range(n_pages)] * 2 + [
                per_seq((LANES, nkv)), per_seq((LANES, nkv)), per_seq((KV_HEADS, rows, HEAD_DIM)),
                per_seq((Q_SLOTS, lp))],
            out_specs=per_seq((KV_HEADS, rows, HEAD_DIM))),
        out_shape=jax.ShapeDtypeStruct((b, KV_HEADS, rows, HEAD_DIM), BF16),
        compiler_params=_cparams("parallel"), name="sample_attention",
    )(page_table, *([cache_k] * n_pages), *([cache_v] * n_pages), kn, vn, q_rows, bias)


def _pad_rows(a, axis, size):
    widths = [(0, 0)] * a.ndim
    widths[axis] = (0, size - a.shape[axis])
    return jnp.pad(a, widths)


def _trunk(x, proj, attn_o, ssm_o, w):
    m = x.shape[0]
    tm = min(512, m)
    mixed = _mix(attn_o, ssm_o, w["pa"], w["ps"], proj, tm, 512)
    h, hn = _outproj(x, mixed, w["out"], w["norm_mlp_g"], min(256, m))
    return _mlp(hn, h, w["up"], w["down"], tm, 1024)


def kernel(x_prompt, x_sample, cache_k, cache_v, cache_idx_k, state_conv, state_ssm, page_table, norm_mix_g, w_in,
           q_norm_g, k_norm_g, idx_ln_g, idx_ln_b, conv_w, conv_b, dt_bias, a_log, d_skip, ssm_norm_g, w_proj_attn,
           w_proj_ssm, w_out, norm_mlp_g, w_up, w_down):
    depth = w_in.shape[0]
    bp, seq = x_prompt.shape[:2]
    db, t_new = x_sample.shape[:2]
    assert depth == 1 and bp == 1, "single-layer, single-prompt configuration"
    n_pages = page_table.shape[1]
    page = cache_k.shape[2]
    past = n_pages * page
    nkv = KV_HEADS * HEAD_DIM
    l = 0

    cuts = [int(c) for c in np.cumsum(SPLITS)[:-1]]
    qc, kc, vc, iqc, ikc, iwc, zc, xbcc, dtc, gac, gbc = jnp.split(w_in[l], cuts, axis=1)
    zeros = lambda n: jnp.zeros((D_MODEL, n), w_in.dtype)
    misc = jnp.concatenate([ikc, iwc, zeros(LANES - IDX_DIM - IDX_HEADS), dtc, zeros(MISC_W - LANES - SSM_HEADS)], axis=1)
    w_p = jnp.concatenate([zc, qc, xbcc, gac, gbc, kc, vc, iqc, misc, zeros(N_PROJ - C_MISC - MISC_W)], axis=1).astype(BF16)
    w = dict(pa=w_proj_attn[l].astype(BF16), ps=w_proj_ssm[l].astype(BF16), out=w_out[l].astype(BF16),
             up=w_up[l].astype(BF16), down=w_down[l].astype(BF16), norm_mlp_g=norm_mlp_g[l])
    sc = _ssm_consts(conv_w[l], conv_b[l], dt_bias[l], a_log[l], d_skip[l], ssm_norm_g[l])
    qg, kg = q_norm_g[l].reshape(1, HEAD_DIM), k_norm_g[l].reshape(1, HEAD_DIM)
    lg = _pad_rows(idx_ln_g[l].reshape(1, IDX_DIM), 1, LANES)
    lb = _pad_rows(idx_ln_b[l].reshape(1, IDX_DIM), 1, LANES)

    def front(x, pos):
        m = x.shape[0]
        xn = _rmsnorm(x, norm_mix_g[l], min(512, m))
        proj = _matmul(xn, w_p, min(512, m), 1024)
        return proj, _prep(proj, _rope_tables(pos), qg, kg, lg, lb)

    xp = x_prompt.reshape(seq, D_MODEL)
    proj_p, pp_ = front(xp, jnp.arange(seq))
    attn_p = _attn_prompt(pp_, min(TOPK_MAX, seq // 4), q_norm_g[l], k_norm_g[l])
    ssm_p, state_p = _ssd_prompt(proj_p, sc)
    y_p = _trunk(xp, proj_p, attn_p, ssm_p, w)

    xs = x_sample.reshape(db * t_new, D_MODEL)
    proj_s, ps_ = front(xs, jnp.tile(past + jnp.arange(t_new), db))
    rh = Q_PER_KV
    q_rows = _pad_rows(ps_["q_bf"].reshape(db, t_new, KV_HEADS, rh, HEAD_DIM).transpose(0, 2, 3, 1, 4), 3, Q_SLOTS)
    q_rows = q_rows.reshape(db, KV_HEADS, rh * Q_SLOTS, HEAD_DIM)
    iq_rows = _pad_rows(ps_["iq_bf"].reshape(db, t_new, IDX_HEADS, IDX_DIM).transpose(0, 2, 1, 3), 2, Q_SLOTS)
    iq_rows = iq_rows.reshape(db, IDX_HEADS * Q_SLOTS, IDX_DIM)
    w_rows = _pad_rows(ps_["wT"].reshape(IDX_HEADS, db, t_new).transpose(1, 0, 2), 2, Q_SLOTS)
    w_rows = jnp.broadcast_to(w_rows.reshape(db, IDX_HEADS * Q_SLOTS, 1), (db, IDX_HEADS * Q_SLOTS, LANES))
    ikn = _pad_rows(ps_["ik_bf"].reshape(db, t_new, IDX_DIM), 1, LANES)
    kn = _pad_rows(ps_["k_f32"].reshape(db, t_new, nkv), 1, LANES)
    v_s = proj_s[:, C_V:C_V + nkv]
    vn = _pad_rows(v_s.reshape(db, t_new, nkv), 1, LANES)
    bias = _scores_sample(page_table, jnp.swapaxes(cache_idx_k[l], 1, 2), iq_rows, w_rows, ikn, t_new,
                          min(TOPK_MAX, (past + t_new) // 4))
    n_pool = cache_k.shape[1]
    assert page == LANES
    o_rows = _attn_sample(page_table, cache_k[l].reshape(n_pool * page * KV_HEADS, HEAD_DIM),
                          cache_v[l].reshape(n_pool * page * KV_HEADS, HEAD_DIM), kn, vn, q_rows, bias)
    attn_s = o_rows.reshape(db, KV_HEADS, rh, Q_SLOTS, HEAD_DIM)[:, :, :, :t_new].transpose(0, 3, 1, 2, 4)
    attn_s = attn_s.reshape(db * t_new, N_HEADS * HEAD_DIM)
    ssm_s, state_s = _ssd_sample(proj_s.reshape(db, t_new, N_PROJ), state_conv[l], state_ssm[l], sc)
    y_s = _trunk(xs, proj_s, attn_s, ssm_s.reshape(db * t_new, D_INNER), w)

    kv_shape = lambda b, t: (1, b, t, KV_HEADS, HEAD_DIM)
    xbc_p = proj_p[seq - (CONV_W - 1):, C_XBC:C_XBC + CONV_DIM]
    xbc_s = proj_s.reshape(db, t_new, N_PROJ)[:, :, C_XBC:C_XBC + CONV_DIM]
    conv_s = jnp.concatenate([state_conv[l], xbc_s], axis=1)[:, t_new:]
    return (y_p.reshape(bp, seq, D_MODEL), y_s.reshape(db, t_new, D_MODEL),
            pp_["k_f32"].reshape(kv_shape(bp, seq)), proj_p[:, C_V:C_V + nkv].reshape(kv_shape(bp, seq)),
            pp_["ik_f32"].reshape(1, bp, seq, IDX_DIM), xbc_p.reshape(1, bp, CONV_W - 1, CONV_DIM),
            state_p.reshape(1, bp, SSM_HEADS, SSM_HEAD_DIM, D_STATE),
            ps_["k_f32"].reshape(kv_shape(db, t_new)), v_s.reshape(kv_shape(db, t_new)),
            ps_["ik_f32"].reshape(1, db, t_new, IDX_DIM), conv_s.reshape(1, db, CONV_W - 1, CONV_DIM),
            state_s.reshape(1, db, SSM_HEADS, SSM_HEAD_DIM, D_STATE))
```

```python
import functools
import math

import jax
import jax.numpy as jnp
import numpy as np
from jax import lax
from jax.experimental import pallas as pl
from jax.experimental.pallas import tpu as pltpu

F32, BF16, I32, I16 = jnp.float32, jnp.bfloat16, jnp.int32, jnp.int16
HALF_BITS = 16

D_MODEL = 2048
N_HEADS = 16
KV_HEADS = 4
HEAD_DIM = 128
Q_PER_KV = N_HEADS // KV_HEADS
IDX_HEADS = 16
IDX_DIM = 64
TOPK_MAX = 256
D_INNER = 4096
SSM_HEAD_DIM = 64
SSM_HEADS = 64
SSM_GROUPS = 8
D_STATE = 128
CONV_W = 4
CONV_DIM = D_INNER + 2 * SSM_GROUPS * D_STATE
SSD_CHUNK = 128
D_FF = 4 * D_MODEL
ROPE_THETA = 10000.0
EPS = 1e-6
SPLITS = (N_HEADS * HEAD_DIM, KV_HEADS * HEAD_DIM, KV_HEADS * HEAD_DIM, IDX_HEADS * IDX_DIM, IDX_DIM, IDX_HEADS,
          D_INNER, CONV_DIM, SSM_HEADS, D_MODEL, D_MODEL)

LANES = 128
SUBLANES = 8
VMEM_LIMIT = 56 * 1024 * 1024

C_Z = 0
C_Q = 4096
C_XBC = 6144
C_GA = 12288
C_GB = 14336
C_K = 16384
C_V = 16896
C_IQ = 17408
C_MISC = 18432
MISC_W = 512
N_PROJ = 19456

INT_MIN = -(2 ** 31)
INT_MAX = 2 ** 31 - 1
NEG_INF_KEY = -2139095041
NEG_BIG = -1e30
Q_TILE = 128
KEY_CHUNK = 1024
LOG2E = 1.4426950408889634


def _cparams(*sem):
    return pltpu.CompilerParams(dimension_semantics=sem, vmem_limit_bytes=VMEM_LIMIT)


def _sortable_key(x):
    bits = pltpu.bitcast(x, I32)
    return bits ^ ((bits >> 31) & INT_MAX)


def _rmsnorm_kernel(x_ref, g_ref, o_ref):
    x = x_ref[...]
    ms = jnp.mean(x * x, axis=-1, keepdims=True)
    o_ref[...] = (x * lax.rsqrt(ms + EPS) * g_ref[...]).astype(o_ref.dtype)


def _rmsnorm(x, g, tm):
    m, d = x.shape
    return pl.pallas_call(
        _rmsnorm_kernel, grid=(m // tm,),
        in_specs=[pl.BlockSpec((tm, d), lambda i: (i, 0)), pl.BlockSpec((1, d), lambda i: (0, 0))],
        out_specs=pl.BlockSpec((tm, d), lambda i: (i, 0)),
        out_shape=jax.ShapeDtypeStruct((m, d), BF16),
        compiler_params=_cparams("parallel"), name="rmsnorm",
    )(x, g.reshape(1, d))


def _matmul_kernel(a_ref, w_ref, o_ref):
    o_ref[...] = jnp.dot(a_ref[...], w_ref[...], preferred_element_type=F32)


def _matmul(a, w, tm, tn):
    m, k = a.shape
    n = w.shape[1]
    return pl.pallas_call(
        _matmul_kernel, grid=(n // tn, m // tm),
        in_specs=[pl.BlockSpec((tm, k), lambda j, i: (i, 0)), pl.BlockSpec((k, tn), lambda j, i: (0, j))],
        out_specs=pl.BlockSpec((tm, tn), lambda j, i: (i, j)),
        out_shape=jax.ShapeDtypeStruct((m, n), F32),
        compiler_params=_cparams("parallel", "parallel"), name="in_proj",
    )(a, w)


def _regroup_plan():
    src = dict(zip(("q", "k", "v", "iq", "ik", "iw", "z", "xbc", "dt", "ga", "gb"),
                   [int(c) for c in np.cumsum((0,) + SPLITS[:-1])]))
    segs = [(C_Z, src["z"], D_INNER), (C_Q, src["q"], N_HEADS * HEAD_DIM), (C_XBC, src["xbc"], CONV_DIM),
            (C_GA, src["ga"], D_MODEL), (C_GB, src["gb"], D_MODEL), (C_K, src["k"], KV_HEADS * HEAD_DIM),
            (C_V, src["v"], KV_HEADS * HEAD_DIM), (C_IQ, src["iq"], IDX_HEADS * IDX_DIM),
            (C_MISC, src["ik"], IDX_DIM + IDX_HEADS), (C_MISC + LANES, src["dt"], SSM_HEADS)]
    blk = np.zeros((N_PROJ // LANES,), np.int32)
    shift = np.zeros_like(blk)
    valid = np.zeros_like(blk)
    for dst, s0, n in segs:
        for off in range(0, n, LANES):
            j = (dst + off) // LANES
            blk[j], shift[j], valid[j] = (s0 + off) // LANES, (s0 + off) % LANES, min(LANES, n - off)
    return blk, shift, valid


def _regroup_kernel(blk_ref, shift_ref, valid_ref, a_ref, b_ref, o_ref):
    del blk_ref
    j = pl.program_id(0)
    r = shift_ref[j]
    lane = lax.broadcasted_iota(I32, a_ref.shape, 1)
    rot = (LANES - r) % LANES
    x = jnp.where(lane < LANES - r, pltpu.roll(a_ref[...], rot, 1), pltpu.roll(b_ref[...], rot, 1))
    o_ref[...] = jnp.where(lane < valid_ref[j], x, 0.0).astype(o_ref.dtype)


def _regroup_w_in(w):
    k, n_src = w.shape
    blk, shift, valid = _regroup_plan()
    last = (n_src - 1) // LANES
    return pl.pallas_call(
        _regroup_kernel,
        grid_spec=pltpu.PrefetchScalarGridSpec(
            num_scalar_prefetch=3, grid=(N_PROJ // LANES,),
            in_specs=[pl.BlockSpec((k, LANES), lambda j, blk, sh, va: (0, blk[j])),
                      pl.BlockSpec((k, LANES), lambda j, blk, sh, va: (0, jnp.minimum(blk[j] + 1, last)))],
            out_specs=pl.BlockSpec((k, LANES), lambda j, blk, sh, va: (0, j))),
        out_shape=jax.ShapeDtypeStruct((k, N_PROJ), BF16),
        compiler_params=_cparams("parallel"), name="regroup_w_in",
    )(jnp.asarray(blk), jnp.asarray(shift), jnp.asarray(valid), w, w)


def _mix_kernel(a_ref, s_ref, wa_ref, ws_ref, ga_ref, gb_ref, o_ref):
    pa = jnp.dot(a_ref[...], wa_ref[...], preferred_element_type=F32)
    ps = jnp.dot(s_ref[...], ws_ref[...], preferred_element_type=F32)
    o_ref[...] = (jax.nn.sigmoid(ga_ref[...]) * pa + jax.nn.sigmoid(gb_ref[...]) * ps).astype(o_ref.dtype)


def _mix(attn_o, ssm_o, wa, ws, proj, tm, tn):
    m = attn_o.shape[0]
    return pl.pallas_call(
        _mix_kernel, grid=(D_MODEL // tn, m // tm),
        in_specs=[pl.BlockSpec((tm, N_HEADS * HEAD_DIM), lambda j, i: (i, 0)),
                  pl.BlockSpec((tm, D_INNER), lambda j, i: (i, 0)),
                  pl.BlockSpec((N_HEADS * HEAD_DIM, tn), lambda j, i: (0, j)),
                  pl.BlockSpec((D_INNER, tn), lambda j, i: (0, j)),
                  pl.BlockSpec((tm, tn), lambda j, i: (i, C_GA // tn + j)),
                  pl.BlockSpec((tm, tn), lambda j, i: (i, C_GB // tn + j))],
        out_specs=pl.BlockSpec((tm, tn), lambda j, i: (i, j)),
        out_shape=jax.ShapeDtypeStruct((m, D_MODEL), BF16),
        compiler_params=_cparams("parallel", "parallel"), name="gated_merge",
    )(attn_o, ssm_o, wa, ws, proj, proj)


def _outproj_kernel(x_ref, mix_ref, w_ref, g_ref, h_ref, hn_ref):
    h = x_ref[...] + jnp.dot(mix_ref[...], w_ref[...], preferred_element_type=F32)
    h_ref[...] = h
    ms = jnp.mean(h * h, axis=-1, keepdims=True)
    hn_ref[...] = (h * lax.rsqrt(ms + EPS) * g_ref[...]).astype(hn_ref.dtype)


def _outproj(x, mixed, w, g, tm):
    m = x.shape[0]
    return pl.pallas_call(
        _outproj_kernel, grid=(m // tm,),
        in_specs=[pl.BlockSpec((tm, D_MODEL), lambda i: (i, 0)), pl.BlockSpec((tm, D_MODEL), lambda i: (i, 0)),
                  pl.BlockSpec((D_MODEL, D_MODEL), lambda i: (0, 0)), pl.BlockSpec((1, D_MODEL), lambda i: (0, 0))],
        out_specs=[pl.BlockSpec((tm, D_MODEL), lambda i: (i, 0)), pl.BlockSpec((tm, D_MODEL), lambda i: (i, 0))],
        out_shape=[jax.ShapeDtypeStruct((m, D_MODEL), F32), jax.ShapeDtypeStruct((m, D_MODEL), BF16)],
        compiler_params=_cparams("parallel"), name="out_proj",
    )(x, mixed, w, g.reshape(1, D_MODEL))


def _mlp_kernel(hn_ref, h_ref, wu_ref, wd_ref, o_ref):
    f = pl.program_id(1)
    u = jnp.dot(hn_ref[...], wu_ref[...], preferred_element_type=F32)
    act = jnp.square(jnp.maximum(u, 0.0)).astype(BF16)
    contrib = jnp.dot(act, wd_ref[...], preferred_element_type=F32)

    @pl.when(f == 0)
    def _():
        o_ref[...] = h_ref[...] + contrib

    @pl.when(f > 0)
    def _():
        o_ref[...] += contrib


def _mlp(hn, h, wu, wd, tm, tf):
    m = h.shape[0]
    return pl.pallas_call(
        _mlp_kernel, grid=(m // tm, D_FF // tf),
        in_specs=[pl.BlockSpec((tm, D_MODEL), lambda i, f: (i, 0)), pl.BlockSpec((tm, D_MODEL), lambda i, f: (i, 0)),
                  pl.BlockSpec((D_MODEL, tf), lambda i, f: (0, f)), pl.BlockSpec((tf, D_MODEL), lambda i, f: (f, 0))],
        out_specs=pl.BlockSpec((tm, D_MODEL), lambda i, f: (i, 0)),
        out_shape=jax.ShapeDtypeStruct((m, D_MODEL), F32),
        compiler_params=_cparams("parallel", "arbitrary"), name="mlp",
    )(hn, h, wu, wd)


def _prep_kernel(q_ref, k_ref, v_ref, iq_ref, misc_ref, ca_ref, sa_ref, cb_ref, sb_ref, qg_ref, kg_ref, lg_ref,
                 lb_ref, kf_ref, kb_ref, ikf_ref, ikb_ref, qn_ref, iqn_ref, qt_ref, iqt_ref, wt_ref, vt_ref):
    ca, sa, cb, sb = ca_ref[...], sa_ref[...], cb_ref[...], sb_ref[...]
    lane = lax.broadcasted_iota(I32, (Q_TILE, LANES), 1)
    first_half = (lane % IDX_DIM) < (IDX_DIM // 2)

    def norm_rope(x, g):
        ms = jnp.mean(x * x, axis=-1, keepdims=True)
        xn = x * lax.rsqrt(ms + EPS) * g
        return xn * ca + pltpu.roll(xn, HEAD_DIM // 2, 1) * sa

    def rope_idx(x):
        rot = jnp.where(first_half, pltpu.roll(x, LANES - IDX_DIM // 2, 1), pltpu.roll(x, IDX_DIM // 2, 1))
        return x * cb + rot * sb

    for h in range(N_HEADS):
        sl = slice(h * HEAD_DIM, (h + 1) * HEAD_DIM)
        qh = norm_rope(q_ref[:, sl], qg_ref[...])
        qn_ref[:, sl] = qh.astype(BF16)
        qt_ref[sl, :] = qh.T.astype(BF16)
    for h in range(KV_HEADS):
        sl = slice(h * HEAD_DIM, (h + 1) * HEAD_DIM)
        kh = norm_rope(k_ref[:, sl], kg_ref[...])
        kf_ref[:, sl] = kh
        kb_ref[:, sl] = kh.astype(BF16)
        vt_ref[sl, :] = v_ref[:, sl].T.astype(BF16)
    for c in range(IDX_HEADS * IDX_DIM // LANES):
        sl = slice(c * LANES, (c + 1) * LANES)
        x = rope_idx(iq_ref[:, sl])
        iqn_ref[:, sl] = x.astype(BF16)
        iqt_ref[sl, :] = x.T.astype(BF16)

    m = misc_ref[:, 0:LANES]
    is_k = lane < IDX_DIM
    mu = jnp.sum(jnp.where(is_k, m, 0.0), axis=-1, keepdims=True) * (1.0 / IDX_DIM)
    xc = jnp.where(is_k, m - mu, 0.0)
    var = jnp.sum(xc * xc, axis=-1, keepdims=True) * (1.0 / IDX_DIM)
    y = rope_idx(xc * lax.rsqrt(var + EPS) * lg_ref[...] + lb_ref[...])
    ikf_ref[...] = y[:, :IDX_DIM]
    ikb_ref[...] = y[:, :IDX_DIM].astype(BF16)
    wt_ref[...] = m.T[IDX_DIM:IDX_DIM + IDX_HEADS, :] * (IDX_HEADS ** -0.5 * IDX_DIM ** -0.5)


def _prep(proj, tabs, qg, kg, lg, lb):
    t = proj.shape[0]
    tq = Q_TILE
    row = lambda w, cb: pl.BlockSpec((tq, w), lambda i, cb=cb: (i, cb))
    const = lambda w: pl.BlockSpec((1, w), lambda i: (0, 0))
    colT = lambda r: pl.BlockSpec((r, tq), lambda i: (0, i))
    nq, nkv, niq = N_HEADS * HEAD_DIM, KV_HEADS * HEAD_DIM, IDX_HEADS * IDX_DIM
    outs = pl.pallas_call(
        _prep_kernel, grid=(t // tq,),
        in_specs=[row(nq, C_Q // nq), row(nkv, C_K // nkv), row(nkv, C_V // nkv), row(niq, C_IQ // niq),
                  row(MISC_W, C_MISC // MISC_W), row(LANES, 0), row(LANES, 0), row(LANES, 0), row(LANES, 0),
                  const(LANES), const(LANES), const(LANES), const(LANES)],
        out_specs=[row(nkv, 0), row(nkv, 0), row(IDX_DIM, 0), row(IDX_DIM, 0), row(nq, 0), row(niq, 0),
                   colT(nq), colT(niq), colT(IDX_HEADS), colT(nkv)],
        out_shape=[jax.ShapeDtypeStruct((t, nkv), F32), jax.ShapeDtypeStruct((t, nkv), BF16),
                   jax.ShapeDtypeStruct((t, IDX_DIM), F32), jax.ShapeDtypeStruct((t, IDX_DIM), BF16),
                   jax.ShapeDtypeStruct((t, nq), BF16), jax.ShapeDtypeStruct((t, niq), BF16),
                   jax.ShapeDtypeStruct((nq, t), BF16), jax.ShapeDtypeStruct((niq, t), BF16),
                   jax.ShapeDtypeStruct((IDX_HEADS, t), F32), jax.ShapeDtypeStruct((nkv, t), BF16)],
        compiler_params=_cparams("parallel"), name="norm_rope_prep",
    )(proj, proj, proj, proj, proj, *tabs, qg, kg, lg, lb)
    return dict(zip(("k_f32", "k_bf", "ik_f32", "ik_bf", "q_bf", "iq_bf", "qT", "iqT", "wT", "vT"), outs))


def _rope_tables(pos):
    def tab(dim):
        half = dim // 2
        inv = ROPE_THETA ** (-jnp.arange(half, dtype=F32) / half)
        ang = pos.astype(F32)[:, None] * inv[None, :]
        c, s = jnp.cos(ang), jnp.sin(ang)
        rep = LANES // dim
        return jnp.tile(jnp.concatenate([c, c], -1), (1, rep)), jnp.tile(jnp.concatenate([-s, s], -1), (1, rep))
    ca, sa = tab(HEAD_DIM)
    cb, sb = tab(IDX_DIM)
    return ca, sa, cb, sb


def _radix_search(count_ge, need, shape, bits):
    lo = -(1 << (bits - 1))
    tau = jnp.full(shape, lo, I32)
    cand = jnp.zeros(shape, I32)
    tau = jnp.where(count_ge(cand) >= need, cand, tau)

    def bit_step(b, tau):
        cand = tau | lax.shift_left(jnp.int32(1), bits - 2 - b)
        return jnp.where(count_ge(cand) >= need, cand, tau)

    return lax.fori_loop(0, bits - 1, bit_step, tau)


def _select_threshold(count_ge, count_gt, count_eq_before, topk, idx_bits, live, tau=None, n_gt=None, n_eq=None):
    if tau is None:
        tau = _radix_search(count_ge, topk, live.shape, 32)
        n_gt = count_gt(tau)
        n_eq = count_ge(tau) - n_gt
    need = topk - n_gt
    tie = jnp.logical_and(jnp.logical_and(n_eq > need, tau != NEG_INF_KEY), live)
    has_tie = jnp.max(tie.astype(I32)) > 0

    def with_ties():
        def step(b, cut):
            cand = cut | lax.shift_left(jnp.int32(1), idx_bits - 1 - b)
            return jnp.where(count_eq_before(tau, cand) < need, cand, cut)
        return lax.fori_loop(0, idx_bits, step, jnp.zeros(live.shape, I32))

    cut = lax.cond(has_tie, with_ties, lambda: jnp.full(live.shape, INT_MAX, I32))
    return tau, cut


def _attn_prompt_kernel(bound_ref, ik_ref, k_ref, vt_ref, iqt_ref, wt_ref, qt_ref, o_ref, keys_ref, hi_ref, lo_ref, acc_ref, *, topk,
                        idx_bits, bounded):
    i = pl.program_id(0)
    ch = KEY_CHUNK
    n_ch = (i * Q_TILE + Q_TILE + ch - 1) // ch
    t_glob = i * Q_TILE + lax.broadcasted_iota(I32, (1, Q_TILE), 1)
    row_iota = lax.broadcasted_iota(I32, (ch, Q_TILE), 0)

    def score_chunk(c, carry):
        base = pl.multiple_of(c * ch, ch)
        ikc = ik_ref[pl.ds(base, ch), :]
        acc = jnp.zeros((ch, Q_TILE), F32)
        for h in range(0, IDX_HEADS, 2):
            pair = jnp.concatenate([iqt_ref[h * IDX_DIM:(h + 1) * IDX_DIM, :],
                                    iqt_ref[(h + 1) * IDX_DIM:(h + 2) * IDX_DIM, :]], axis=1)
            d = jnp.dot(ikc, pair, preferred_element_type=F32)
            acc = acc + jnp.maximum(d[:, :Q_TILE], 0.0) * wt_ref[h:h + 1, :]
            acc = acc + jnp.maximum(d[:, Q_TILE:], 0.0) * wt_ref[h + 1:h + 2, :]
        acc = jnp.where(base + row_iota <= t_glob, acc, -jnp.inf)
        key = _sortable_key(acc)
        keys_ref[pl.ds(base, ch), :] = key
        hi_ref[pl.ds(base, ch), :] = (key >> HALF_BITS).astype(I16)
        return carry

    lax.fori_loop(0, n_ch, score_chunk, 0)

    pack = 2 * SUBLANES

    def count16_ge(ref, cand):
        cand16 = cand.astype(I16)

        def body(c, cnt):
            blk = ref[pl.ds(pl.multiple_of(c * ch, ch), ch), :]
            hit = jnp.where(blk >= cand16, jnp.ones((), BF16), jnp.zeros((), BF16))
            parts = [hit[r * pack:(r + 1) * pack] for r in range(ch // pack)]
            while len(parts) > 1:
                parts = [a + b for a, b in zip(parts[::2], parts[1::2])]
            return cnt + parts[0].astype(F32)

        cnt = lax.fori_loop(0, n_ch, body, jnp.zeros((pack, Q_TILE), F32))
        return jnp.sum(cnt, axis=0, keepdims=True)

    lane_shape = (1, Q_TILE)
    half_min, half_max = -(1 << (HALF_BITS - 1)), (1 << (HALF_BITS - 1)) - 1
    tau_hi = _radix_search(lambda cand: count16_ge(hi_ref, cand), float(topk), lane_shape, HALF_BITS)
    n_above = jnp.where(tau_hi >= half_max, 0.0, count16_ge(hi_ref, jnp.minimum(tau_hi + 1, half_max)))

    def low_half_chunk(c, carry):
        base = pl.multiple_of(c * ch, ch)
        key = keys_ref[pl.ds(base, ch), :]
        low = (key & ((1 << HALF_BITS) - 1)) + half_min
        lo_ref[pl.ds(base, ch), :] = jnp.where((key >> HALF_BITS) == tau_hi, low, half_min).astype(I16)
        return carry

    lax.fori_loop(0, n_ch, low_half_chunk, 0)
    tau_lo = _radix_search(lambda cand: count16_ge(lo_ref, cand), float(topk) - n_above, lane_shape, HALF_BITS)
    tau_full = lax.shift_left(tau_hi, HALF_BITS) | (tau_lo - half_min)

    n_hi_eq = count16_ge(hi_ref, tau_hi) - n_above
    n_lo_ge = jnp.where(tau_lo <= half_min, n_hi_eq, count16_ge(lo_ref, tau_lo))
    n_lo_gt = jnp.where(tau_lo >= half_max, 0.0, count16_ge(lo_ref, jnp.minimum(tau_lo + 1, half_max)))

    def count_eq_before(tau, cut):
        def body(c, cnt):
            base = pl.multiple_of(c * ch, ch)
            hit = jnp.where(keys_ref[pl.ds(base, ch), :] == tau, (base + row_iota < cut).astype(I32), 0)
            return cnt + jnp.sum(hit.reshape(ch // SUBLANES, SUBLANES, Q_TILE), axis=0)
        cnt = lax.fori_loop(0, n_ch, body, jnp.zeros((SUBLANES, Q_TILE), I32))
        return jnp.sum(cnt, axis=0, keepdims=True)

    tau, cut = _select_threshold(None, None, count_eq_before, topk, idx_bits, jnp.ones(lane_shape, jnp.bool_),
                                 tau=tau_full, n_gt=(n_above + n_lo_gt).astype(I32),
                                 n_eq=(n_lo_ge - n_lo_gt).astype(I32))

    offset = -bound_ref[0] if bounded else 0.0

    def selection_bias(base):
        blk = keys_ref[pl.ds(base, ch), :]
        idx = base + row_iota
        sel = jnp.where(blk == tau, (idx <= cut).astype(I32), (blk > tau).astype(I32))
        sel = jnp.where(idx <= t_glob, sel, 0)
        return jnp.where(sel > 0, offset, NEG_BIG).astype(F32)

    scale2 = HEAD_DIM ** -0.5 * LOG2E
    qgs = [jnp.concatenate([qt_ref[(g * Q_PER_KV + r) * HEAD_DIM:(g * Q_PER_KV + r + 1) * HEAD_DIM, :]
                            for r in range(Q_PER_KV)], axis=1) for g in range(KV_HEADS)]
    acc_ref[...] = jnp.zeros_like(acc_ref)

    def attn_chunk(c, carry):
        base = pl.multiple_of(c * ch, ch)
        bias = jnp.concatenate([selection_bias(base)] * Q_PER_KV, axis=1)
        out = []
        for g in range(KV_HEADS):
            kc = k_ref[pl.ds(base, ch), g * HEAD_DIM:(g + 1) * HEAD_DIM]
            v_t = vt_ref[g * HEAD_DIM:(g + 1) * HEAD_DIM, pl.ds(base, ch)]
            s = jnp.dot(kc, qgs[g], preferred_element_type=F32) + bias
            if bounded:
                p = jnp.exp2(s * scale2)
                acc_ref[g] += jnp.dot(v_t, p.astype(BF16), preferred_element_type=F32)
                out.append(carry[g] + jnp.sum(p, axis=0, keepdims=True))
            else:
                m, l = carry[g]
                m_new = jnp.maximum(m, jnp.max(s, axis=0, keepdims=True))
                p = jnp.exp2((s - m_new) * scale2)
                alpha = jnp.exp2((m - m_new) * scale2)
                acc_ref[g] = acc_ref[g] * alpha + jnp.dot(v_t, p.astype(BF16), preferred_element_type=F32)
                out.append((m_new, l * alpha + jnp.sum(p, axis=0, keepdims=True)))
        return tuple(out)

    l0 = jnp.zeros((1, Q_PER_KV * Q_TILE), F32)
    init = l0 if bounded else (jnp.full((1, Q_PER_KV * Q_TILE), NEG_BIG, F32), l0)
    fin = lax.fori_loop(0, n_ch, attn_chunk, (init,) * KV_HEADS)
    for g in range(KV_HEADS):
        o = acc_ref[g] / (fin[g] if bounded else fin[g][1])
        for r in range(Q_PER_KV):
            h = g * Q_PER_KV + r
            o_ref[:, h * HEAD_DIM:(h + 1) * HEAD_DIM] = o[:, r * Q_TILE:(r + 1) * Q_TILE].T.astype(o_ref.dtype)


MAX_BOUNDED_LOGIT = 40.0


def _attn_prompt_call(p, topk, bound, bounded):
    t = p["k_bf"].shape[0]
    assert t % KEY_CHUNK == 0, "prompt length must be a multiple of the key chunk"
    nq, nkv, niq = N_HEADS * HEAD_DIM, KV_HEADS * HEAD_DIM, IDX_HEADS * IDX_DIM
    full = lambda shape: pl.BlockSpec(shape, lambda i: (0, 0))
    colT = lambda r: pl.BlockSpec((r, Q_TILE), lambda i: (0, i))
    kern = functools.partial(_attn_prompt_kernel, topk=topk, idx_bits=max(1, (t - 1).bit_length()), bounded=bounded)
    return pl.pallas_call(
        kern, grid=(t // Q_TILE,),
        in_specs=[pl.BlockSpec(memory_space=pltpu.SMEM), full((t, IDX_DIM)), full((t, nkv)), full((nkv, t)),
                  colT(niq), colT(IDX_HEADS), colT(nq)],
        out_specs=pl.BlockSpec((Q_TILE, nq), lambda i: (i, 0)),
        out_shape=jax.ShapeDtypeStruct((t, nq), BF16),
        scratch_shapes=[pltpu.VMEM((t, Q_TILE), I32), pltpu.VMEM((t, Q_TILE), I16), pltpu.VMEM((t, Q_TILE), I16),
                        pltpu.VMEM((KV_HEADS, HEAD_DIM, Q_PER_KV * Q_TILE), F32)],
        compiler_params=_cparams("arbitrary"), name="prompt_attention_bounded" if bounded else "prompt_attention",
    )(bound, p["ik_bf"], p["k_bf"], p["vT"], p["iqT"], p["wT"], p["qT"])


def _attn_prompt(p, topk, q_gain, k_gain):
    bound = (HEAD_DIM * 1.02) * jnp.max(jnp.abs(q_gain)) * jnp.max(jnp.abs(k_gain))
    bound = bound.reshape(1).astype(F32)
    return lax.cond(bound[0] * HEAD_DIM ** -0.5 <= MAX_BOUNDED_LOGIT,
                    lambda: _attn_prompt_call(p, topk, bound, True),
                    lambda: _attn_prompt_call(p, topk, bound, False))


def _softplus(x):
    return jnp.maximum(x, 0.0) + jnp.log1p(jnp.exp(-jnp.abs(x)))


def _split3(v):
    v1 = v.astype(BF16)
    r1 = v - v1.astype(F32)
    v2 = r1.astype(BF16)
    v3 = (r1 - v2.astype(F32)).astype(BF16)
    return v1, v2, v3


def _dot3(parts, e):
    return sum(jnp.dot(p, e, preferred_element_type=F32) for p in parts)


LANE_BLK = 512


def _conv_silu(ext_ref, row0, rows, cw_ref, cb_ref, xa_ref):
    for j in range(CONV_DIM // LANE_BLK):
        sl = slice(j * LANE_BLK, (j + 1) * LANE_BLK)
        acc = ext_ref[row0:row0 + rows, sl] * cw_ref[0:1, sl]
        for i in range(1, CONV_W):
            acc = acc + ext_ref[row0 + i:row0 + i + rows, sl] * cw_ref[i:i + 1, sl]
        acc = acc + cb_ref[:, sl]
        xa_ref[:, sl] = acc * jax.nn.sigmoid(acc)


def _gated_norm(y_ref, xa_ref, z_ref, dsk_ref, ng_ref, o_ref, zidx):
    for j in range(SSM_GROUPS):
        sl = slice(j * LANE_BLK, (j + 1) * LANE_BLK)
        y = y_ref[:, sl] + dsk_ref[:, sl] * xa_ref[:, sl]
        zz = z_ref[zidx + (slice(None), sl)] if zidx else z_ref[:, sl]
        y = y * (zz * jax.nn.sigmoid(zz))
        ms = jnp.mean(y * y, axis=-1, keepdims=True)
        res = (y * lax.rsqrt(ms + EPS) * ng_ref[:, sl]).astype(o_ref.dtype)
        if zidx:
            o_ref[zidx + (slice(None), sl)] = res
        else:
            o_ref[:, sl] = res


def _ssd_prompt_kernel(xbc_ref, z_ref, misc_ref, cw_ref, cb_ref, dtb_ref, alog_ref, dsk_ref, ng_ref, exp_ref, tri_ref,
                       o_ref, st_ref, ext_ref, xa_ref, xlo_ref, xhi_ref, xw_ref, ecs_ref, y_ref, state_ref):
    c = pl.program_id(0)
    q = SSD_CHUNK
    pad = SUBLANES

    @pl.when(c == 0)
    def _():
        ext_ref[0:pad, :] = jnp.zeros((pad, CONV_DIM), F32)
        state_ref[...] = jnp.zeros_like(state_ref)

    ext_ref[pad:pad + q, :] = xbc_ref[...]
    _conv_silu(ext_ref, pad - (CONV_W - 1), q, cw_ref, cb_ref, xa_ref)
    ext_ref[0:pad, :] = ext_ref[q:q + pad, :]

    dt = _softplus(misc_ref[:, LANES:2 * LANES] + dtb_ref[...])
    da = dt * (-jnp.exp(alog_ref[...]))
    cs = jnp.dot(tri_ref[...], da, precision=lax.Precision.HIGHEST, preferred_element_type=F32)
    cs_t = cs.T
    ecs = jnp.exp(cs)
    dec_end = jnp.exp(cs[q - 1:q, :] - cs)
    parts = _split3(jnp.concatenate([dt, dt * dec_end, ecs], axis=0))
    lane = lax.broadcasted_iota(I32, (q, LANE_BLK), 1)
    lo = (lane % LANES) < SSM_HEAD_DIM
    for j in range(D_INNER // LANE_BLK):
        sl = slice(j * LANE_BLK, (j + 1) * LANE_BLK)
        ex = _dot3(parts, exp_ref[:, sl])
        xs = xa_ref[:, sl]
        xdt = xs * ex[0:q]
        xlo_ref[:, sl] = jnp.where(lo, xdt, 0.0).astype(BF16)
        xhi_ref[:, sl] = jnp.where(lo, 0.0, xdt).astype(BF16)
        xw_ref[:, sl] = (xs * ex[q:2 * q]).astype(BF16)
        ecs_ref[:, sl] = ex[2 * q:3 * q]

    causal = lax.broadcasted_iota(I32, (q, q), 0) >= lax.broadcasted_iota(I32, (q, q), 1)
    heads_per_group = SSM_HEADS // SSM_GROUPS
    for g in range(SSM_GROUPS):
        b0 = D_INNER + g * D_STATE
        c0 = D_INNER + SSM_GROUPS * D_STATE + g * D_STATE
        bg = xa_ref[:, b0:b0 + D_STATE]
        cg = xa_ref[:, c0:c0 + D_STATE].astype(BF16)
        cbm = lax.dot_general(cg, bg.astype(BF16), (((1,), (1,)), ((), ())), preferred_element_type=F32)
        bg_t = bg.T.astype(BF16)
        for j in range(heads_per_group // 2):
            pidx = g * (heads_per_group // 2) + j
            sl = slice(pidx * LANES, (pidx + 1) * LANES)
            yd = None
            for hh, x_ref in ((0, xlo_ref), (1, xhi_ref)):
                h = 2 * pidx + hh
                lm = jnp.exp(jnp.where(causal, cs[:, h:h + 1] - cs_t[h:h + 1, :], -jnp.inf))
                t = jnp.dot((cbm * lm).astype(BF16), x_ref[:, sl], preferred_element_type=F32)
                yd = t if yd is None else yd + t
            st = state_ref[pidx]
            yo = jnp.dot(cg, st.astype(BF16), preferred_element_type=F32) * ecs_ref[:, sl]
            y_ref[:, sl] = yd + yo
            state_ref[pidx] = st * ecs_ref[q - 1:q, sl] + jnp.dot(bg_t, xw_ref[:, sl], preferred_element_type=F32)

    _gated_norm(y_ref, xa_ref, z_ref, dsk_ref, ng_ref, o_ref, ())

    @pl.when(c == pl.num_programs(0) - 1)
    def _():
        for p in range(SSM_HEADS // 2):
            st_ref[2 * p:2 * p + 2] = state_ref[p].T.reshape(2, SSM_HEAD_DIM, D_STATE)


def _ssm_consts(conv_w, conv_b, dt_bias, a_log, d_skip, ssm_norm_g):
    pad_h = lambda v: jnp.pad(v.reshape(1, SSM_HEADS), ((0, 0), (0, LANES - SSM_HEADS)))
    head_of_lane = np.arange(D_INNER) // SSM_HEAD_DIM
    expand = jnp.asarray((np.arange(LANES)[:, None] == head_of_lane[None, :]), BF16)
    return dict(cw=conv_w, cb=conv_b.reshape(1, CONV_DIM), dtb=pad_h(dt_bias), alog=pad_h(a_log),
                dsk=jnp.repeat(d_skip, SSM_HEAD_DIM).reshape(1, D_INNER), ng=ssm_norm_g.reshape(1, D_INNER),
                expand=expand)


def _ssd_prompt(proj, sc):
    t = proj.shape[0]
    q = SSD_CHUNK
    assert t % q == 0
    tri = jnp.asarray(np.tril(np.ones((q, q), np.float32)))
    row = lambda w, cb: pl.BlockSpec((q, w), lambda i, cb=cb: (i, cb))
    const = lambda r, w: pl.BlockSpec((r, w), lambda i: (0, 0))
    return pl.pallas_call(
        _ssd_prompt_kernel, grid=(t // q,),
        in_specs=[row(CONV_DIM, C_XBC // CONV_DIM), row(D_INNER, C_Z // D_INNER), row(MISC_W, C_MISC // MISC_W),
                  const(CONV_W, CONV_DIM), const(1, CONV_DIM), const(1, LANES), const(1, LANES),
                  const(1, D_INNER), const(1, D_INNER), const(LANES, D_INNER), const(q, q)],
        out_specs=[pl.BlockSpec((q, D_INNER), lambda i: (i, 0)),
                   pl.BlockSpec((SSM_HEADS, SSM_HEAD_DIM, D_STATE), lambda i: (0, 0, 0))],
        out_shape=[jax.ShapeDtypeStruct((t, D_INNER), BF16),
                   jax.ShapeDtypeStruct((SSM_HEADS, SSM_HEAD_DIM, D_STATE), F32)],
        scratch_shapes=[pltpu.VMEM((q + 2 * SUBLANES, CONV_DIM), F32), pltpu.VMEM((q, CONV_DIM), F32),
                        pltpu.VMEM((q, D_INNER), BF16), pltpu.VMEM((q, D_INNER), BF16), pltpu.VMEM((q, D_INNER), BF16),
                        pltpu.VMEM((q, D_INNER), F32), pltpu.VMEM((q, D_INNER), F32),
                        pltpu.VMEM((SSM_HEADS // 2, D_STATE, LANES), F32)],
        compiler_params=_cparams("arbitrary"), name="ssd_prompt",
    )(proj, proj, proj, sc["cw"], sc["cb"], sc["dtb"], sc["alog"], sc["dsk"], sc["ng"], sc["expand"], tri)


def _ssd_sample_kernel(xbc_ref, z_ref, misc_ref, cst_ref, st_ref, cw_ref, cb_ref, dtb_ref, alog_ref, dsk_ref, ng_ref,
                       exp_ref, gsum_ref, gexp_ref, o_ref, nst_ref, ext_ref, xa_ref, y_ref, *, t):
    nt = (((1,), (1,)), ((), ()))
    tn = (((0,), (0,)), ((), ()))
    ext_ref[...] = jnp.zeros_like(ext_ref)
    ext_ref[0:CONV_W - 1, :] = cst_ref[0]
    ext_ref[CONV_W - 1:CONV_W - 1 + t, :] = xbc_ref[0]
    _conv_silu(ext_ref, 0, t, cw_ref, cb_ref, xa_ref)

    dt = _softplus(misc_ref[0][:, LANES:2 * LANES] + dtb_ref[...])
    da = dt * (-jnp.exp(alog_ref[...]))
    rows = [da[0:1]]
    for i in range(1, t):
        rows.append(rows[-1] + da[i:i + 1])
    cs = jnp.concatenate(rows, axis=0)
    ecs = jnp.exp(cs)
    dec_end = jnp.exp(cs[t - 1:t] - cs)
    parts = _split3(jnp.concatenate([dt, dt * dec_end, ecs, cs], axis=0))

    pairs = [(l, s) for l in range(t) for s in range(l + 1)]
    n_b = SSM_GROUPS * D_STATE
    bm = xa_ref[:, D_INNER:D_INNER + n_b]
    cm = xa_ref[:, D_INNER + n_b:D_INNER + 2 * n_b]
    prod = jnp.concatenate([cm[l:l + 1] * bm[s:s + 1] for l, s in pairs], axis=0)
    cb_parts = _split3(_dot3(_split3(prod), gsum_ref[...]))

    heads_per_group = SSM_HEADS // SSM_GROUPS
    for g in range(SSM_GROUPS):
        sl = slice(g * LANE_BLK, (g + 1) * LANE_BLK)
        ex = _dot3(parts, exp_ref[:, sl])
        cbe = _dot3(cb_parts, gexp_ref[:, sl])
        xs = xa_ref[:, sl]
        xdt, xw, ecs_e, cs_e = xs * ex[0:t], xs * ex[t:2 * t], ex[2 * t:3 * t], ex[3 * t:4 * t]
        yrows = []
        for l in range(t):
            acc = None
            for s in range(l + 1):
                pi = pairs.index((l, s))
                coef = cbe[pi:pi + 1] if s == l else cbe[pi:pi + 1] * jnp.exp(cs_e[l:l + 1] - cs_e[s:s + 1])
                term = coef * xdt[s:s + 1]
                acc = term if acc is None else acc + term
            yrows.append(acc)
        sg = st_ref[0, g * heads_per_group:(g + 1) * heads_per_group].reshape(LANE_BLK, D_STATE)
        cg = xa_ref[:, D_INNER + n_b + g * D_STATE:D_INNER + n_b + (g + 1) * D_STATE].astype(BF16)
        yo = lax.dot_general(cg, sg.astype(BF16), nt, preferred_element_type=F32) * ecs_e
        y_ref[:, sl] = jnp.concatenate(yrows, axis=0) + yo
        bg = xa_ref[:, D_INNER + g * D_STATE:D_INNER + (g + 1) * D_STATE]
        zpad = jnp.zeros((SUBLANES - t, LANE_BLK), F32)
        upd = lax.dot_general(jnp.concatenate([xw, zpad], axis=0).astype(BF16),
                              jnp.concatenate([bg, zpad[:, :D_STATE]], axis=0).astype(BF16), tn,
                              preferred_element_type=F32)
        for r in range(heads_per_group):
            h = g * heads_per_group + r
            nst_ref[0, h] = st_ref[0, h] * ecs[t - 1:t, h:h + 1] + upd[r * SSM_HEAD_DIM:(r + 1) * SSM_HEAD_DIM]

    _gated_norm(y_ref, xa_ref, z_ref, dsk_ref, ng_ref, o_ref, (0,))


def _ssd_sample(proj3, conv_state, ssm_state, sc):
    b, t = proj3.shape[:2]
    assert t + CONV_W - 1 <= SUBLANES
    n_pairs = t * (t + 1) // 2
    n_b = SSM_GROUPS * D_STATE
    gsum = jnp.asarray(np.arange(n_b)[:, None] // D_STATE == np.arange(LANES)[None, :], BF16)
    gexp = jnp.asarray(np.arange(LANES)[:, None] == (np.arange(D_INNER) // LANE_BLK)[None, :], BF16)
    row = lambda w, cb: pl.BlockSpec((1, t, w), lambda i, cb=cb: (i, 0, cb))
    const = lambda r, w: pl.BlockSpec((r, w), lambda i: (0, 0))
    st_spec = pl.BlockSpec((1, SSM_HEADS, SSM_HEAD_DIM, D_STATE), lambda i: (i, 0, 0, 0))
    kern = functools.partial(_ssd_sample_kernel, t=t)
    del n_pairs
    return pl.pallas_call(
        kern, grid=(b,),
        in_specs=[row(CONV_DIM, C_XBC // CONV_DIM), row(D_INNER, C_Z // D_INNER), row(MISC_W, C_MISC // MISC_W),
                  pl.BlockSpec((1, CONV_W - 1, CONV_DIM), lambda i: (i, 0, 0)), st_spec,
                  const(CONV_W, CONV_DIM), const(1, CONV_DIM), const(1, LANES), const(1, LANES),
                  const(1, D_INNER), const(1, D_INNER), const(LANES, D_INNER), const(n_b, LANES),
                  const(LANES, D_INNER)],
        out_specs=[pl.BlockSpec((1, t, D_INNER), lambda i: (i, 0, 0)), st_spec],
        out_shape=[jax.ShapeDtypeStruct((b, t, D_INNER), BF16),
                   jax.ShapeDtypeStruct((b, SSM_HEADS, SSM_HEAD_DIM, D_STATE), F32)],
        scratch_shapes=[pltpu.VMEM((SUBLANES, CONV_DIM), F32), pltpu.VMEM((t, CONV_DIM), F32),
                        pltpu.VMEM((t, D_INNER), F32)],
        compiler_params=_cparams("parallel"), name="ssd_sample",
    )(proj3, proj3, proj3, conv_state, ssm_state, sc["cw"], sc["cb"], sc["dtb"], sc["alog"], sc["dsk"], sc["ng"],
      sc["expand"], gsum, gexp)


SEQS_PER_STEP = 8
Q_SLOTS = SUBLANES
NT_DIMS = (((1,), (1,)), ((), ()))


def _scores_sample_kernel(pt_ref, *refs, past, t_new, topk, idx_bits):
    del pt_ref
    gs = SEQS_PER_STEP
    page_refs = refs[:gs]
    iq_ref, w_ref, ikn_ref, bias_ref, keys_ref = refs[gs:]
    j = pl.program_id(1)
    lp = past + LANES

    def head_sum(d, w):
        s = jnp.maximum(d, 0.0) * w
        return jnp.sum(s.reshape(IDX_HEADS, Q_SLOTS, LANES), axis=0)

    for k in range(gs):
        d = jnp.dot(iq_ref[k], page_refs[k][0].astype(BF16), preferred_element_type=F32)
        keys_ref[k, :, pl.ds(pl.multiple_of(j * LANES, LANES), LANES)] = _sortable_key(head_sum(d, w_ref[k]))

    @pl.when(j == pl.num_programs(1) - 1)
    def _():
        tq = lax.broadcasted_iota(I32, (Q_SLOTS, LANES), 0)
        jn = lax.broadcasted_iota(I32, (Q_SLOTS, LANES), 1)
        ok_new = jnp.logical_and(jn <= tq, jn < t_new)
        for k in range(gs):
            d = lax.dot_general(iq_ref[k], ikn_ref[k], NT_DIMS, preferred_element_type=F32)
            s = head_sum(d, w_ref[k])
            keys_ref[k, :, past:lp] = _sortable_key(jnp.where(ok_new, s, -jnp.inf))

        idx = lax.broadcasted_iota(I32, (gs, Q_SLOTS, lp), 2)
        slot = lax.broadcasted_iota(I32, (gs, Q_SLOTS, lp), 1)
        cnt = lambda hit: jnp.sum(hit.astype(I32), axis=2, keepdims=True)
        count_ge = lambda cand: cnt(keys_ref[...] >= cand)
        count_gt = lambda cand: cnt(keys_ref[...] > cand)
        count_eq_before = lambda tau, cutc: cnt(jnp.where(keys_ref[...] == tau, (idx < cutc).astype(I32), 0))
        live = lax.broadcasted_iota(I32, (gs, Q_SLOTS, 1), 1) < t_new
        tau, cut = _select_threshold(count_ge, count_gt, count_eq_before, topk, idx_bits, live)
        keys = keys_ref[...]
        sel = jnp.where(keys == tau, (idx <= cut).astype(I32), (keys > tau).astype(I32))
        new_ok = jnp.where(idx - past <= slot, (idx - past < t_new).astype(I32), 0)
        valid = jnp.where(idx < past, 1, new_ok)
        bias_ref[...] = jnp.where(sel * valid > 0, 0.0, NEG_BIG).astype(F32)


def _scores_sample(page_table, cache_idx_t, iq_rows, w_rows, ikn, t_new, topk):
    b, n_pages = page_table.shape
    page = cache_idx_t.shape[2]
    assert page == LANES and b % SEQS_PER_STEP == 0
    past = n_pages * page
    lp = past + LANES
    gs = SEQS_PER_STEP
    page_spec = lambda k: pl.BlockSpec((1, IDX_DIM, page), lambda sg, j, pt, k=k: (pt[sg * gs + k, j], 0, 0))
    grp = lambda r, w: pl.BlockSpec((gs, r, w), lambda sg, j, pt: (sg, 0, 0))
    rows = IDX_HEADS * Q_SLOTS
    kern = functools.partial(_scores_sample_kernel, past=past, t_new=t_new, topk=topk,
                             idx_bits=max(1, (lp - 1).bit_length()))
    return pl.pallas_call(
        kern,
        grid_spec=pltpu.PrefetchScalarGridSpec(
            num_scalar_prefetch=1, grid=(b // gs, n_pages),
            in_specs=[page_spec(k) for k in range(gs)] + [grp(rows, IDX_DIM), grp(rows, LANES), grp(LANES, IDX_DIM)],
            out_specs=grp(Q_SLOTS, lp),
            scratch_shapes=[pltpu.VMEM((gs, Q_SLOTS, lp), I32)]),
        out_shape=jax.ShapeDtypeStruct((b, Q_SLOTS, lp), F32),
        compiler_params=_cparams("parallel", "arbitrary"), name="sample_index_select",
    )(page_table, *([cache_idx_t] * gs), iq_rows, w_rows, ikn)


def _attn_sample_kernel(pt_ref, *refs, n_pages):
    del pt_ref
    k_refs, v_refs = refs[:n_pages], refs[n_pages:2 * n_pages]
    kn_ref, vn_ref, q_ref, bias_ref, o_ref = refs[2 * n_pages:]
    scale2 = HEAD_DIM ** -0.5 * LOG2E
    past = n_pages * LANES
    bias_r = jnp.concatenate([bias_ref[0]] * Q_PER_KV, axis=0)
    for g in range(KV_HEADS):
        sl = slice(g * HEAD_DIM, (g + 1) * HEAD_DIM)
        head = lambda ref: ref[pl.ds(g, LANES, stride=KV_HEADS), :].astype(BF16)
        qg = q_ref[0, g]
        sc = [lax.dot_general(qg, head(r), NT_DIMS, preferred_element_type=F32) for r in k_refs]
        sc.append(lax.dot_general(qg, kn_ref[0][:, sl].astype(BF16), NT_DIMS, preferred_element_type=F32))
        s = jnp.concatenate(sc, axis=1) + bias_r
        m = jnp.max(s, axis=-1, keepdims=True)
        p = jnp.exp2((s - m) * scale2)
        l = jnp.sum(p, axis=-1, keepdims=True)
        pb = p.astype(BF16)
        acc = jnp.dot(pb[:, past:], vn_ref[0][:, sl].astype(BF16), preferred_element_type=F32)
        for j, r in enumerate(v_refs):
            acc = acc + jnp.dot(pb[:, j * LANES:(j + 1) * LANES], head(r), preferred_element_type=F32)
        o_ref[0, g] = (acc / l).astype(o_ref.dtype)


def _attn_sample(page_table, cache_k, cache_v, kn, vn, q_rows, bias):
    b, n_pages = page_table.shape
    nkv = KV_HEADS * HEAD_DIM
    rows = Q_PER_KV * Q_SLOTS
    lp = bias.shape[-1]
    page_spec = lambda k: pl.BlockSpec((LANES * KV_HEADS, HEAD_DIM), lambda i, pt, k=k: (pt[i, k], 0))
    per_seq = lambda shape: pl.BlockSpec((1,) + shape, lambda i, pt: (i,) + (0,) * len(shape))
    kern = functools.partial(_attn_sample_kernel, n_pages=n_pages)
    return pl.pallas_call(
        kern,
        grid_spec=pltpu.PrefetchScalarGridSpec(
            num_scalar_prefetch=1, grid=(b,),
            in_specs=[page_spec(k) for k in range(n_pages)] * 2 + [
                per_seq((LANES, nkv)), per_seq((LANES, nkv)), per_seq((KV_HEADS, rows, HEAD_DIM)),
                per_seq((Q_SLOTS, lp))],
            out_specs=per_seq((KV_HEADS, rows, HEAD_DIM))),
        out_shape=jax.ShapeDtypeStruct((b, KV_HEADS, rows, HEAD_DIM), BF16),
        compiler_params=_cparams("parallel"), name="sample_attention",
    )(page_table, *([cache_k] * n_pages), *([cache_v] * n_pages), kn, vn, q_rows, bias)


def _pad_rows(a, axis, size):
    widths = [(0, 0)] * a.ndim
    widths[axis] = (0, size - a.shape[axis])
    return jnp.pad(a, widths)


def _trunk(x, proj, attn_o, ssm_o, w):
    m = x.shape[0]
    tm = min(512, m)
    mixed = _mix(attn_o, ssm_o, w["pa"], w["ps"], proj, tm, 512)
    h, hn = _outproj(x, mixed, w["out"], w["norm_mlp_g"], min(256, m))
    return _mlp(hn, h, w["up"], w["down"], tm, 1024)


def kernel(x_prompt, x_sample, cache_k, cache_v, cache_idx_k, state_conv, state_ssm, page_table, norm_mix_g, w_in,
           q_norm_g, k_norm_g, idx_ln_g, idx_ln_b, conv_w, conv_b, dt_bias, a_log, d_skip, ssm_norm_g, w_proj_attn,
           w_proj_ssm, w_out, norm_mlp_g, w_up, w_down):
    depth = w_in.shape[0]
    bp, seq = x_prompt.shape[:2]
    db, t_new = x_sample.shape[:2]
    assert depth == 1 and bp == 1, "single-layer, single-prompt configuration"
    n_pages = page_table.shape[1]
    page = cache_k.shape[2]
    past = n_pages * page
    nkv = KV_HEADS * HEAD_DIM
    l = 0

    w_p = _regroup_w_in(w_in[l])
    w = dict(pa=w_proj_attn[l].astype(BF16), ps=w_proj_ssm[l].astype(BF16), out=w_out[l].astype(BF16),
             up=w_up[l].astype(BF16), down=w_down[l].astype(BF16), norm_mlp_g=norm_mlp_g[l])
    sc = _ssm_consts(conv_w[l], conv_b[l], dt_bias[l], a_log[l], d_skip[l], ssm_norm_g[l])
    qg, kg = q_norm_g[l].reshape(1, HEAD_DIM), k_norm_g[l].reshape(1, HEAD_DIM)
    lg = _pad_rows(idx_ln_g[l].reshape(1, IDX_DIM), 1, LANES)
    lb = _pad_rows(idx_ln_b[l].reshape(1, IDX_DIM), 1, LANES)

    def front(x, pos):
        m = x.shape[0]
        xn = _rmsnorm(x, norm_mix_g[l], min(512, m))
        proj = _matmul(xn, w_p, min(512, m), 1024)
        return proj, _prep(proj, _rope_tables(pos), qg, kg, lg, lb)

    xp = x_prompt.reshape(seq, D_MODEL)
    proj_p, pp_ = front(xp, jnp.arange(seq))
    attn_p = _attn_prompt(pp_, min(TOPK_MAX, seq // 4), q_norm_g[l], k_norm_g[l])
    ssm_p, state_p = _ssd_prompt(proj_p, sc)
    y_p = _trunk(xp, proj_p, attn_p, ssm_p, w)

    xs = x_sample.reshape(db * t_new, D_MODEL)
    proj_s, ps_ = front(xs, jnp.tile(past + jnp.arange(t_new), db))
    rh = Q_PER_KV
    q_rows = _pad_rows(ps_["q_bf"].reshape(db, t_new, KV_HEADS, rh, HEAD_DIM).transpose(0, 2, 3, 1, 4), 3, Q_SLOTS)
    q_rows = q_rows.reshape(db, KV_HEADS, rh * Q_SLOTS, HEAD_DIM)
    iq_rows = _pad_rows(ps_["iq_bf"].reshape(db, t_new, IDX_HEADS, IDX_DIM).transpose(0, 2, 1, 3), 2, Q_SLOTS)
    iq_rows = iq_rows.reshape(db, IDX_HEADS * Q_SLOTS, IDX_DIM)
    w_rows = _pad_rows(ps_["wT"].reshape(IDX_HEADS, db, t_new).transpose(1, 0, 2), 2, Q_SLOTS)
    w_rows = jnp.broadcast_to(w_rows.reshape(db, IDX_HEADS * Q_SLOTS, 1), (db, IDX_HEADS * Q_SLOTS, LANES))
    ikn = _pad_rows(ps_["ik_bf"].reshape(db, t_new, IDX_DIM), 1, LANES)
    kn = _pad_rows(ps_["k_f32"].reshape(db, t_new, nkv), 1, LANES)
    v_s = proj_s[:, C_V:C_V + nkv]
    vn = _pad_rows(v_s.reshape(db, t_new, nkv), 1, LANES)
    bias = _scores_sample(page_table, jnp.swapaxes(cache_idx_k[l], 1, 2), iq_rows, w_rows, ikn, t_new,
                          min(TOPK_MAX, (past + t_new) // 4))
    n_pool = cache_k.shape[1]
    assert page == LANES
    o_rows = _attn_sample(page_table, cache_k[l].reshape(n_pool * page * KV_HEADS, HEAD_DIM),
                          cache_v[l].reshape(n_pool * page * KV_HEADS, HEAD_DIM), kn, vn, q_rows, bias)
    attn_s = o_rows.reshape(db, KV_HEADS, rh, Q_SLOTS, HEAD_DIM)[:, :, :, :t_new].transpose(0, 3, 1, 2, 4)
    attn_s = attn_s.reshape(db * t_new, N_HEADS * HEAD_DIM)
    ssm_s, state_s = _ssd_sample(proj_s.reshape(db, t_new, N_PROJ), state_conv[l], state_ssm[l], sc)
    y_s = _trunk(xs, proj_s, attn_s, ssm_s.reshape(db * t_new, D_INNER), w)

    kv_shape = lambda b, t: (1, b, t, KV_HEADS, HEAD_DIM)
    xbc_p = proj_p[seq - (CONV_W - 1):, C_XBC:C_XBC + CONV_DIM]
    xbc_s = proj_s.reshape(db, t_new, N_PROJ)[:, :, C_XBC:C_XBC + CONV_DIM]
    conv_s = jnp.concatenate([state_conv[l], xbc_s], axis=1)[:, t_new:]
    return (y_p.reshape(bp, seq, D_MODEL), y_s.reshape(db, t_new, D_MODEL),
            pp_["k_f32"].reshape(kv_shape(bp, seq)), proj_p[:, C_V:C_V + nkv].reshape(kv_shape(bp, seq)),
            pp_["ik_f32"].reshape(1, bp, seq, IDX_DIM), xbc_p.reshape(1, bp, CONV_W - 1, CONV_DIM),
            state_p.reshape(1, bp, SSM_HEADS, SSM_HEAD_DIM, D_STATE),
            ps_["k_f32"].reshape(kv_shape(db, t_new)), v_s.reshape(kv_shape(db, t_new)),
            ps_["ik_f32"].reshape(1, db, t_new, IDX_DIM), conv_s.reshape(1, db, CONV_W - 1, CONV_DIM),
            state_s.reshape(1, db, SSM_HEADS, SSM_HEAD_DIM, D_STATE))
```

```python
import functools
import math

import jax
import jax.numpy as jnp
import numpy as np
from jax import lax
from jax.experimental import pallas as pl
from jax.experimental.pallas import tpu as pltpu

F32, BF16, I32 = jnp.float32, jnp.bfloat16, jnp.int32

D_MODEL = 2048
N_HEADS = 16
KV_HEADS = 4
HEAD_DIM = 128
Q_PER_KV = N_HEADS // KV_HEADS
IDX_HEADS = 16
IDX_DIM = 64
TOPK_MAX = 256
D_INNER = 4096
SSM_HEAD_DIM = 64
SSM_HEADS = 64
SSM_GROUPS = 8
D_STATE = 128
CONV_W = 4
CONV_DIM = D_INNER + 2 * SSM_GROUPS * D_STATE
SSD_CHUNK = 128
D_FF = 4 * D_MODEL
ROPE_THETA = 10000.0
EPS = 1e-6
SPLITS = (N_HEADS * HEAD_DIM, KV_HEADS * HEAD_DIM, KV_HEADS * HEAD_DIM, IDX_HEADS * IDX_DIM, IDX_DIM, IDX_HEADS,
          D_INNER, CONV_DIM, SSM_HEADS, D_MODEL, D_MODEL)

LANES = 128
SUBLANES = 8
VMEM_LIMIT = 56 * 1024 * 1024

C_Z = 0
C_Q = 4096
C_XBC = 6144
C_GA = 12288
C_GB = 14336
C_K = 16384
C_V = 16896
C_IQ = 17408
C_MISC = 18432
MISC_DT = 256
REGROUP_BLK = 256
MISC_W = 512
N_PROJ = 19456

INT_MIN = -(2 ** 31)
INT_MAX = 2 ** 31 - 1
NEG_INF_KEY = -2139095041
NEG_BIG = -1e30
Q_TILE = 128
KEY_CHUNK = 1024
COUNT_CHUNK = 512
LOG2E = 1.4426950408889634


def _cparams(*sem):
    return pltpu.CompilerParams(dimension_semantics=sem, vmem_limit_bytes=VMEM_LIMIT)


def _sortable_key(x):
    bits = pltpu.bitcast(x, I32)
    return bits ^ ((bits >> 31) & INT_MAX)


def _rmsnorm_kernel(x_ref, g_ref, o_ref):
    x = x_ref[...]
    ms = jnp.mean(x * x, axis=-1, keepdims=True)
    o_ref[...] = (x * lax.rsqrt(ms + EPS) * g_ref[...]).astype(o_ref.dtype)


def _rmsnorm(x, g, tm):
    m, d = x.shape
    return pl.pallas_call(
        _rmsnorm_kernel, grid=(m // tm,),
        in_specs=[pl.BlockSpec((tm, d), lambda i: (i, 0)), pl.BlockSpec((1, d), lambda i: (0, 0))],
        out_specs=pl.BlockSpec((tm, d), lambda i: (i, 0)),
        out_shape=jax.ShapeDtypeStruct((m, d), BF16),
        compiler_params=_cparams("parallel"), name="rmsnorm",
    )(x, g.reshape(1, d))


def _matmul_kernel(a_ref, w_ref, o_ref):
    o_ref[...] = jnp.dot(a_ref[...], w_ref[...], preferred_element_type=F32)


def _matmul(a, w, tm, tn):
    m, k = a.shape
    n = w.shape[1]
    return pl.pallas_call(
        _matmul_kernel, grid=(n // tn, m // tm),
        in_specs=[pl.BlockSpec((tm, k), lambda j, i: (i, 0)), pl.BlockSpec((k, tn), lambda j, i: (0, j))],
        out_specs=pl.BlockSpec((tm, tn), lambda j, i: (i, j)),
        out_shape=jax.ShapeDtypeStruct((m, n), F32),
        compiler_params=_cparams("parallel", "parallel"), name="in_proj",
    )(a, w)


def _regroup_plan():
    src = dict(zip(("q", "k", "v", "iq", "ik", "iw", "z", "xbc", "dt", "ga", "gb"),
                   [int(c) for c in np.cumsum((0,) + SPLITS[:-1])]))
    segs = [(C_Z, src["z"], D_INNER), (C_Q, src["q"], N_HEADS * HEAD_DIM), (C_XBC, src["xbc"], CONV_DIM),
            (C_GA, src["ga"], D_MODEL), (C_GB, src["gb"], D_MODEL), (C_K, src["k"], KV_HEADS * HEAD_DIM),
            (C_V, src["v"], KV_HEADS * HEAD_DIM), (C_IQ, src["iq"], IDX_HEADS * IDX_DIM),
            (C_MISC, src["ik"], IDX_DIM + IDX_HEADS), (C_MISC + MISC_DT, src["dt"], SSM_HEADS)]
    start = np.zeros((N_PROJ // REGROUP_BLK,), np.int32)
    valid = np.zeros_like(start)
    for dst, s0, n in segs:
        assert dst % REGROUP_BLK == 0
        for off in range(0, n, REGROUP_BLK):
            j = (dst + off) // REGROUP_BLK
            start[j], valid[j] = s0 + off, min(REGROUP_BLK, n - off)
    return start, valid


def _regroup_kernel(start_ref, valid_ref, a_ref, o_ref):
    del start_ref
    row = lax.broadcasted_iota(I32, a_ref.shape, 0)
    x = jnp.where(row < valid_ref[pl.program_id(0)], a_ref[...], 0.0)
    o_ref[...] = x.T.astype(o_ref.dtype)


def _regroup_w_in(w_t):
    n_src, k = w_t.shape
    start, valid = _regroup_plan()
    assert int(start.max()) + REGROUP_BLK <= n_src and all(int(s) % SUBLANES == 0 for s in start)
    start = start // SUBLANES
    return pl.pallas_call(
        _regroup_kernel,
        grid_spec=pltpu.PrefetchScalarGridSpec(
            num_scalar_prefetch=2, grid=(N_PROJ // REGROUP_BLK,),
            in_specs=[pl.BlockSpec((pl.Element(REGROUP_BLK), pl.Element(k)),
                                   lambda j, st, va: (st[j] * SUBLANES, 0))],
            out_specs=pl.BlockSpec((k, REGROUP_BLK), lambda j, st, va: (0, j))),
        out_shape=jax.ShapeDtypeStruct((k, N_PROJ), BF16),
        compiler_params=_cparams("parallel"), name="regroup_w_in",
    )(jnp.asarray(start), jnp.asarray(valid), w_t)


def _mix_kernel(a_ref, s_ref, wa_ref, ws_ref, ga_ref, gb_ref, o_ref):
    pa = jnp.dot(a_ref[...], wa_ref[...], preferred_element_type=F32)
    ps = jnp.dot(s_ref[...], ws_ref[...], preferred_element_type=F32)
    o_ref[...] = (jax.nn.sigmoid(ga_ref[...]) * pa + jax.nn.sigmoid(gb_ref[...]) * ps).astype(o_ref.dtype)


def _mix(attn_o, ssm_o, wa, ws, proj, tm, tn):
    m = attn_o.shape[0]
    return pl.pallas_call(
        _mix_kernel, grid=(D_MODEL // tn, m // tm),
        in_specs=[pl.BlockSpec((tm, N_HEADS * HEAD_DIM), lambda j, i: (i, 0)),
                  pl.BlockSpec((tm, D_INNER), lambda j, i: (i, 0)),
                  pl.BlockSpec((N_HEADS * HEAD_DIM, tn), lambda j, i: (0, j)),
                  pl.BlockSpec((D_INNER, tn), lambda j, i: (0, j)),
                  pl.BlockSpec((tm, tn), lambda j, i: (i, C_GA // tn + j)),
                  pl.BlockSpec((tm, tn), lambda j, i: (i, C_GB // tn + j))],
        out_specs=pl.BlockSpec((tm, tn), lambda j, i: (i, j)),
        out_shape=jax.ShapeDtypeStruct((m, D_MODEL), BF16),
        compiler_params=_cparams("parallel", "parallel"), name="gated_merge",
    )(attn_o, ssm_o, wa, ws, proj, proj)


def _outproj_kernel(x_ref, mix_ref, w_ref, g_ref, h_ref, hn_ref):
    h = x_ref[...] + jnp.dot(mix_ref[...], w_ref[...], preferred_element_type=F32)
    h_ref[...] = h
    ms = jnp.mean(h * h, axis=-1, keepdims=True)
    hn_ref[...] = (h * lax.rsqrt(ms + EPS) * g_ref[...]).astype(hn_ref.dtype)


def _outproj(x, mixed, w, g, tm):
    m = x.shape[0]
    return pl.pallas_call(
        _outproj_kernel, grid=(m // tm,),
        in_specs=[pl.BlockSpec((tm, D_MODEL), lambda i: (i, 0)), pl.BlockSpec((tm, D_MODEL), lambda i: (i, 0)),
                  pl.BlockSpec((D_MODEL, D_MODEL), lambda i: (0, 0)), pl.BlockSpec((1, D_MODEL), lambda i: (0, 0))],
        out_specs=[pl.BlockSpec((tm, D_MODEL), lambda i: (i, 0)), pl.BlockSpec((tm, D_MODEL), lambda i: (i, 0))],
        out_shape=[jax.ShapeDtypeStruct((m, D_MODEL), F32), jax.ShapeDtypeStruct((m, D_MODEL), BF16)],
        compiler_params=_cparams("parallel"), name="out_proj",
    )(x, mixed, w, g.reshape(1, D_MODEL))


def _mlp_kernel(hn_ref, h_ref, wu_ref, wd_ref, o_ref):
    f = pl.program_id(1)
    u = jnp.dot(hn_ref[...], wu_ref[...], preferred_element_type=F32)
    act = jnp.square(jnp.maximum(u, 0.0)).astype(BF16)
    contrib = jnp.dot(act, wd_ref[...], preferred_element_type=F32)

    @pl.when(f == 0)
    def _():
        o_ref[...] = h_ref[...] + contrib

    @pl.when(f > 0)
    def _():
        o_ref[...] += contrib


def _mlp(hn, h, wu, wd, tm, tf):
    m = h.shape[0]
    return pl.pallas_call(
        _mlp_kernel, grid=(m // tm, D_FF // tf),
        in_specs=[pl.BlockSpec((tm, D_MODEL), lambda i, f: (i, 0)), pl.BlockSpec((tm, D_MODEL), lambda i, f: (i, 0)),
                  pl.BlockSpec((D_MODEL, tf), lambda i, f: (0, f)), pl.BlockSpec((tf, D_MODEL), lambda i, f: (f, 0))],
        out_specs=pl.BlockSpec((tm, D_MODEL), lambda i, f: (i, 0)),
        out_shape=jax.ShapeDtypeStruct((m, D_MODEL), F32),
        compiler_params=_cparams("parallel", "arbitrary"), name="mlp",
    )(hn, h, wu, wd)


def _prep_kernel(q_ref, k_ref, v_ref, iq_ref, misc_ref, ca_ref, sa_ref, cb_ref, sb_ref, qg_ref, kg_ref, lg_ref,
                 lb_ref, kf_ref, kb_ref, ikf_ref, ikb_ref, qn_ref, iqn_ref, qt_ref, iqt_ref, wt_ref, vt_ref):
    ca, sa, cb, sb = ca_ref[...], sa_ref[...], cb_ref[...], sb_ref[...]
    lane = lax.broadcasted_iota(I32, (Q_TILE, LANES), 1)
    first_half = (lane % IDX_DIM) < (IDX_DIM // 2)

    def norm_rope(x, g):
        ms = jnp.mean(x * x, axis=-1, keepdims=True)
        xn = x * lax.rsqrt(ms + EPS) * g
        return xn * ca + pltpu.roll(xn, HEAD_DIM // 2, 1) * sa

    def rope_idx(x):
        rot = jnp.where(first_half, pltpu.roll(x, LANES - IDX_DIM // 2, 1), pltpu.roll(x, IDX_DIM // 2, 1))
        return x * cb + rot * sb

    for h in range(N_HEADS):
        sl = slice(h * HEAD_DIM, (h + 1) * HEAD_DIM)
        qh = norm_rope(q_ref[:, sl], qg_ref[...])
        qn_ref[:, sl] = qh.astype(BF16)
        qt_ref[sl, :] = qh.T.astype(BF16)
    for h in range(KV_HEADS):
        sl = slice(h * HEAD_DIM, (h + 1) * HEAD_DIM)
        kh = norm_rope(k_ref[:, sl], kg_ref[...])
        kf_ref[:, sl] = kh
        kb_ref[:, sl] = kh.astype(BF16)
        vt_ref[sl, :] = v_ref[:, sl].T.astype(BF16)
    for c in range(IDX_HEADS * IDX_DIM // LANES):
        sl = slice(c * LANES, (c + 1) * LANES)
        x = rope_idx(iq_ref[:, sl])
        iqn_ref[:, sl] = x.astype(BF16)
        iqt_ref[sl, :] = x.T.astype(BF16)

    m = misc_ref[:, 0:LANES]
    is_k = lane < IDX_DIM
    mu = jnp.sum(jnp.where(is_k, m, 0.0), axis=-1, keepdims=True) * (1.0 / IDX_DIM)
    xc = jnp.where(is_k, m - mu, 0.0)
    var = jnp.sum(xc * xc, axis=-1, keepdims=True) * (1.0 / IDX_DIM)
    y = rope_idx(xc * lax.rsqrt(var + EPS) * lg_ref[...] + lb_ref[...])
    ikf_ref[...] = y[:, :IDX_DIM]
    ikb_ref[...] = y[:, :IDX_DIM].astype(BF16)
    wt_ref[...] = m.T[IDX_DIM:IDX_DIM + IDX_HEADS, :] * (IDX_HEADS ** -0.5 * IDX_DIM ** -0.5)


def _prep(proj, tabs, qg, kg, lg, lb):
    t = proj.shape[0]
    tq = Q_TILE
    row = lambda w, cb: pl.BlockSpec((tq, w), lambda i, cb=cb: (i, cb))
    const = lambda w: pl.BlockSpec((1, w), lambda i: (0, 0))
    colT = lambda r: pl.BlockSpec((r, tq), lambda i: (0, i))
    nq, nkv, niq = N_HEADS * HEAD_DIM, KV_HEADS * HEAD_DIM, IDX_HEADS * IDX_DIM
    outs = pl.pallas_call(
        _prep_kernel, grid=(t // tq,),
        in_specs=[row(nq, C_Q // nq), row(nkv, C_K // nkv), row(nkv, C_V // nkv), row(niq, C_IQ // niq),
                  row(MISC_W, C_MISC // MISC_W), row(LANES, 0), row(LANES, 0), row(LANES, 0), row(LANES, 0),
                  const(LANES), const(LANES), const(LANES), const(LANES)],
        out_specs=[row(nkv, 0), row(nkv, 0), row(IDX_DIM, 0), row(IDX_DIM, 0), row(nq, 0), row(niq, 0),
                   colT(nq), colT(niq), colT(IDX_HEADS), colT(nkv)],
        out_shape=[jax.ShapeDtypeStruct((t, nkv), F32), jax.ShapeDtypeStruct((t, nkv), BF16),
                   jax.ShapeDtypeStruct((t, IDX_DIM), F32), jax.ShapeDtypeStruct((t, IDX_DIM), BF16),
                   jax.ShapeDtypeStruct((t, nq), BF16), jax.ShapeDtypeStruct((t, niq), BF16),
                   jax.ShapeDtypeStruct((nq, t), BF16), jax.ShapeDtypeStruct((niq, t), BF16),
                   jax.ShapeDtypeStruct((IDX_HEADS, t), F32), jax.ShapeDtypeStruct((nkv, t), BF16)],
        compiler_params=_cparams("parallel"), name="norm_rope_prep",
    )(proj, proj, proj, proj, proj, *tabs, qg, kg, lg, lb)
    return dict(zip(("k_f32", "k_bf", "ik_f32", "ik_bf", "q_bf", "iq_bf", "qT", "iqT", "wT", "vT"), outs))


def _rope_tables(pos):
    def tab(dim):
        half = dim // 2
        inv = ROPE_THETA ** (-jnp.arange(half, dtype=F32) / half)
        ang = pos.astype(F32)[:, None] * inv[None, :]
        c, s = jnp.cos(ang), jnp.sin(ang)
        rep = LANES // dim
        return jnp.tile(jnp.concatenate([c, c], -1), (1, rep)), jnp.tile(jnp.concatenate([-s, s], -1), (1, rep))
    ca, sa = tab(HEAD_DIM)
    cb, sb = tab(IDX_DIM)
    return ca, sa, cb, sb


def _select_threshold(count_ge, count_gt, count_eq_before, topk, idx_bits, live):
    tau = jnp.full(live.shape, INT_MIN, I32)
    cand = jnp.zeros(live.shape, I32)
    tau = jnp.where(count_ge(cand) >= topk, cand, tau)

    def bit_step(b, tau):
        cand = tau | lax.shift_left(jnp.int32(1), 30 - b)
        return jnp.where(count_ge(cand) >= topk, cand, tau)

    tau = lax.fori_loop(0, 31, bit_step, tau)
    n_gt = count_gt(tau)
    n_eq = count_ge(tau) - n_gt
    need = topk - n_gt
    tie = jnp.logical_and(jnp.logical_and(n_eq > need, tau != NEG_INF_KEY), live)
    has_tie = jnp.max(tie.astype(I32)) > 0

    def with_ties():
        def step(b, cut):
            cand = cut | lax.shift_left(jnp.int32(1), idx_bits - 1 - b)
            return jnp.where(count_eq_before(tau, cand) < need, cand, cut)
        return lax.fori_loop(0, idx_bits, step, jnp.zeros(live.shape, I32))

    cut = lax.cond(has_tie, with_ties, lambda: jnp.full(live.shape, INT_MAX, I32))
    return tau, cut


def _attn_prompt_kernel(bound_ref, ik_ref, k_ref, vt_ref, iqt_ref, wt_ref, qt_ref, o_ref, keys_ref, acc_ref, *, topk,
                        idx_bits, bounded):
    i = pl.program_id(0)
    ch = KEY_CHUNK
    n_ch = (i * Q_TILE + Q_TILE + ch - 1) // ch
    t_glob = i * Q_TILE + lax.broadcasted_iota(I32, (1, Q_TILE), 1)
    row_iota = lax.broadcasted_iota(I32, (ch, Q_TILE), 0)

    def score_chunk(c, carry):
        base = pl.multiple_of(c * ch, ch)
        ikc = ik_ref[pl.ds(base, ch), :]
        acc = jnp.zeros((ch, Q_TILE), F32)
        for h in range(0, IDX_HEADS, 2):
            pair = jnp.concatenate([iqt_ref[h * IDX_DIM:(h + 1) * IDX_DIM, :],
                                    iqt_ref[(h + 1) * IDX_DIM:(h + 2) * IDX_DIM, :]], axis=1)
            d = jnp.dot(ikc, pair, preferred_element_type=F32)
            acc = acc + jnp.maximum(d[:, :Q_TILE], 0.0) * wt_ref[h:h + 1, :]
            acc = acc + jnp.maximum(d[:, Q_TILE:], 0.0) * wt_ref[h + 1:h + 2, :]
        acc = jnp.where(base + row_iota <= t_glob, acc, -jnp.inf)
        keys_ref[pl.ds(base, ch), :] = _sortable_key(acc)
        return carry

    lax.fori_loop(0, n_ch, score_chunk, 0)

    def _count(pred):
        cnt_iota = lax.broadcasted_iota(I32, (COUNT_CHUNK, Q_TILE), 0)

        def body(c, cnt):
            for u in range(ch // COUNT_CHUNK):
                base = pl.multiple_of(c * ch + u * COUNT_CHUNK, COUNT_CHUNK)
                hit = pred(keys_ref[pl.ds(base, COUNT_CHUNK), :], base + cnt_iota).astype(I32)
                cnt = cnt + jnp.sum(hit.reshape(COUNT_CHUNK // SUBLANES, SUBLANES, Q_TILE), axis=0)
            return cnt

        cnt = lax.fori_loop(0, n_ch, body, jnp.zeros((SUBLANES, Q_TILE), I32))
        return jnp.sum(cnt, axis=0, keepdims=True)

    tau, cut = _select_threshold(
        lambda cand: _count(lambda blk, idx: blk >= cand), lambda cand: _count(lambda blk, idx: blk > cand),
        lambda tau, cut: _count(lambda blk, idx: jnp.where(blk == tau, (idx < cut).astype(I32), 0)),
        topk, idx_bits, jnp.ones((1, Q_TILE), jnp.bool_))

    offset = -bound_ref[0] if bounded else 0.0

    def bias_chunk(c, carry):
        base = pl.multiple_of(c * ch, ch)
        blk = keys_ref[pl.ds(base, ch), :]
        idx = base + row_iota
        sel = jnp.where(blk == tau, (idx <= cut).astype(I32), (blk > tau).astype(I32))
        sel = jnp.where(idx <= t_glob, sel, 0)
        keys_ref[pl.ds(base, ch), :] = pltpu.bitcast(jnp.where(sel > 0, offset, NEG_BIG).astype(F32), I32)
        return carry

    lax.fori_loop(0, n_ch, bias_chunk, 0)

    scale2 = HEAD_DIM ** -0.5 * LOG2E
    qgs = [jnp.concatenate([qt_ref[(g * Q_PER_KV + r) * HEAD_DIM:(g * Q_PER_KV + r + 1) * HEAD_DIM, :]
                            for r in range(Q_PER_KV)], axis=1) for g in range(KV_HEADS)]
    acc_ref[...] = jnp.zeros_like(acc_ref)

    def attn_chunk(c, carry):
        base = pl.multiple_of(c * ch, ch)
        bias = jnp.concatenate([pltpu.bitcast(keys_ref[pl.ds(base, ch), :], F32)] * Q_PER_KV, axis=1)
        out = []
        for g in range(KV_HEADS):
            kc = k_ref[pl.ds(base, ch), g * HEAD_DIM:(g + 1) * HEAD_DIM]
            v_t = vt_ref[g * HEAD_DIM:(g + 1) * HEAD_DIM, pl.ds(base, ch)]
            s = jnp.dot(kc, qgs[g], preferred_element_type=F32) + bias
            if bounded:
                p = jnp.exp2(s * scale2)
                acc_ref[g] += jnp.dot(v_t, p.astype(BF16), preferred_element_type=F32)
                out.append(carry[g] + jnp.sum(p, axis=0, keepdims=True))
            else:
                m, l = carry[g]
                m_new = jnp.maximum(m, jnp.max(s, axis=0, keepdims=True))
                p = jnp.exp2((s - m_new) * scale2)
                alpha = jnp.exp2((m - m_new) * scale2)
                acc_ref[g] = acc_ref[g] * alpha + jnp.dot(v_t, p.astype(BF16), preferred_element_type=F32)
                out.append((m_new, l * alpha + jnp.sum(p, axis=0, keepdims=True)))
        return tuple(out)

    l0 = jnp.zeros((1, Q_PER_KV * Q_TILE), F32)
    init = l0 if bounded else (jnp.full((1, Q_PER_KV * Q_TILE), NEG_BIG, F32), l0)
    fin = lax.fori_loop(0, n_ch, attn_chunk, (init,) * KV_HEADS)
    for g in range(KV_HEADS):
        o = acc_ref[g] / (fin[g] if bounded else fin[g][1])
        for r in range(Q_PER_KV):
            h = g * Q_PER_KV + r
            o_ref[:, h * HEAD_DIM:(h + 1) * HEAD_DIM] = o[:, r * Q_TILE:(r + 1) * Q_TILE].T.astype(o_ref.dtype)


MAX_BOUNDED_LOGIT = 40.0


def _attn_prompt_call(p, topk, bound, bounded):
    t = p["k_bf"].shape[0]
    assert t % KEY_CHUNK == 0, "prompt length must be a multiple of the key chunk"
    nq, nkv, niq = N_HEADS * HEAD_DIM, KV_HEADS * HEAD_DIM, IDX_HEADS * IDX_DIM
    full = lambda shape: pl.BlockSpec(shape, lambda i: (0, 0))
    colT = lambda r: pl.BlockSpec((r, Q_TILE), lambda i: (0, i))
    kern = functools.partial(_attn_prompt_kernel, topk=topk, idx_bits=max(1, (t - 1).bit_length()), bounded=bounded)
    return pl.pallas_call(
        kern, grid=(t // Q_TILE,),
        in_specs=[pl.BlockSpec(memory_space=pltpu.SMEM), full((t, IDX_DIM)), full((t, nkv)), full((nkv, t)),
                  colT(niq), colT(IDX_HEADS), colT(nq)],
        out_specs=pl.BlockSpec((Q_TILE, nq), lambda i: (i, 0)),
        out_shape=jax.ShapeDtypeStruct((t, nq), BF16),
        scratch_shapes=[pltpu.VMEM((t, Q_TILE), I32), pltpu.VMEM((KV_HEADS, HEAD_DIM, Q_PER_KV * Q_TILE), F32)],
        compiler_params=_cparams("arbitrary"), name="prompt_attention_bounded" if bounded else "prompt_attention",
    )(bound, p["ik_bf"], p["k_bf"], p["vT"], p["iqT"], p["wT"], p["qT"])


def _attn_prompt(p, topk, q_gain, k_gain):
    bound = (HEAD_DIM * 1.02) * jnp.max(jnp.abs(q_gain)) * jnp.max(jnp.abs(k_gain))
    bound = bound.reshape(1).astype(F32)
    return lax.cond(bound[0] * HEAD_DIM ** -0.5 <= MAX_BOUNDED_LOGIT,
                    lambda: _attn_prompt_call(p, topk, bound, True),
                    lambda: _attn_prompt_call(p, topk, bound, False))


def _softplus(x):
    return jnp.maximum(x, 0.0) + jnp.log1p(jnp.exp(-jnp.abs(x)))


def _split3(v):
    v1 = v.astype(BF16)
    r1 = v - v1.astype(F32)
    v2 = r1.astype(BF16)
    v3 = (r1 - v2.astype(F32)).astype(BF16)
    return v1, v2, v3


def _dot3(parts, e):
    return sum(jnp.dot(p, e, preferred_element_type=F32) for p in parts)


LANE_BLK = 512


def _conv_silu(ext_ref, row0, rows, cw_ref, cb_ref, xa_ref):
    for j in range(CONV_DIM // LANE_BLK):
        sl = slice(j * LANE_BLK, (j + 1) * LANE_BLK)
        acc = ext_ref[row0:row0 + rows, sl] * cw_ref[0:1, sl]
        for i in range(1, CONV_W):
            acc = acc + ext_ref[row0 + i:row0 + i + rows, sl] * cw_ref[i:i + 1, sl]
        acc = acc + cb_ref[:, sl]
        xa_ref[:, sl] = acc * jax.nn.sigmoid(acc)


def _gated_norm(y_ref, xa_ref, z_ref, dsk_ref, ng_ref, o_ref, zidx):
    for j in range(SSM_GROUPS):
        sl = slice(j * LANE_BLK, (j + 1) * LANE_BLK)
        y = y_ref[:, sl] + dsk_ref[:, sl] * xa_ref[:, sl]
        zz = z_ref[zidx + (slice(None), sl)] if zidx else z_ref[:, sl]
        y = y * (zz * jax.nn.sigmoid(zz))
        ms = jnp.mean(y * y, axis=-1, keepdims=True)
        res = (y * lax.rsqrt(ms + EPS) * ng_ref[:, sl]).astype(o_ref.dtype)
        if zidx:
            o_ref[zidx + (slice(None), sl)] = res
        else:
            o_ref[:, sl] = res


def _ssd_prompt_kernel(xbc_ref, z_ref, misc_ref, cw_ref, cb_ref, dtb_ref, alog_ref, dsk_ref, ng_ref, exp_ref, tri_ref,
                       o_ref, st_ref, ext_ref, xa_ref, xlo_ref, xhi_ref, xw_ref, ecs_ref, y_ref, state_ref):
    c = pl.program_id(0)
    q = SSD_CHUNK
    pad = SUBLANES

    @pl.when(c == 0)
    def _():
        ext_ref[0:pad, :] = jnp.zeros((pad, CONV_DIM), F32)
        state_ref[...] = jnp.zeros_like(state_ref)

    ext_ref[pad:pad + q, :] = xbc_ref[...]
    _conv_silu(ext_ref, pad - (CONV_W - 1), q, cw_ref, cb_ref, xa_ref)
    ext_ref[0:pad, :] = ext_ref[q:q + pad, :]

    dt = _softplus(misc_ref[:, MISC_DT:MISC_DT + LANES] + dtb_ref[...])
    da = dt * (-jnp.exp(alog_ref[...]))
    cs = jnp.dot(tri_ref[...], da, precision=lax.Precision.HIGHEST, preferred_element_type=F32)
    cs_t = cs.T
    ecs = jnp.exp(cs)
    dec_end = jnp.exp(cs[q - 1:q, :] - cs)
    parts = _split3(jnp.concatenate([dt, dt * dec_end, ecs], axis=0))
    lane = lax.broadcasted_iota(I32, (q, LANE_BLK), 1)
    lo = (lane % LANES) < SSM_HEAD_DIM
    for j in range(D_INNER // LANE_BLK):
        sl = slice(j * LANE_BLK, (j + 1) * LANE_BLK)
        ex = _dot3(parts, exp_ref[:, sl])
        xs = xa_ref[:, sl]
        xdt = xs * ex[0:q]
        xlo_ref[:, sl] = jnp.where(lo, xdt, 0.0).astype(BF16)
        xhi_ref[:, sl] = jnp.where(lo, 0.0, xdt).astype(BF16)
        xw_ref[:, sl] = (xs * ex[q:2 * q]).astype(BF16)
        ecs_ref[:, sl] = ex[2 * q:3 * q]

    causal = lax.broadcasted_iota(I32, (q, q), 0) >= lax.broadcasted_iota(I32, (q, q), 1)
    heads_per_group = SSM_HEADS // SSM_GROUPS
    for g in range(SSM_GROUPS):
        b0 = D_INNER + g * D_STATE
        c0 = D_INNER + SSM_GROUPS * D_STATE + g * D_STATE
        bg = xa_ref[:, b0:b0 + D_STATE]
        cg = xa_ref[:, c0:c0 + D_STATE].astype(BF16)
        cbm = lax.dot_general(cg, bg.astype(BF16), (((1,), (1,)), ((), ())), preferred_element_type=F32)
        bg_t = bg.T.astype(BF16)
        for j in range(heads_per_group // 2):
            pidx = g * (heads_per_group // 2) + j
            sl = slice(pidx * LANES, (pidx + 1) * LANES)
            yd = None
            for hh, x_ref in ((0, xlo_ref), (1, xhi_ref)):
                h = 2 * pidx + hh
                lm = jnp.exp(jnp.where(causal, cs[:, h:h + 1] - cs_t[h:h + 1, :], -jnp.inf))
                t = jnp.dot((cbm * lm).astype(BF16), x_ref[:, sl], preferred_element_type=F32)
                yd = t if yd is None else yd + t
            st = state_ref[pidx]
            yo = jnp.dot(cg, st.astype(BF16), preferred_element_type=F32) * ecs_ref[:, sl]
            y_ref[:, sl] = yd + yo
            state_ref[pidx] = st * ecs_ref[q - 1:q, sl] + jnp.dot(bg_t, xw_ref[:, sl], preferred_element_type=F32)

    _gated_norm(y_ref, xa_ref, z_ref, dsk_ref, ng_ref, o_ref, ())

    @pl.when(c == pl.num_programs(0) - 1)
    def _():
        for p in range(SSM_HEADS // 2):
            st_ref[2 * p:2 * p + 2] = state_ref[p].T.reshape(2, SSM_HEAD_DIM, D_STATE)


def _ssm_consts(conv_w, conv_b, dt_bias, a_log, d_skip, ssm_norm_g):
    pad_h = lambda v: jnp.pad(v.reshape(1, SSM_HEADS), ((0, 0), (0, LANES - SSM_HEADS)))
    head_of_lane = np.arange(D_INNER) // SSM_HEAD_DIM
    expand = jnp.asarray((np.arange(LANES)[:, None] == head_of_lane[None, :]), BF16)
    return dict(cw=conv_w, cb=conv_b.reshape(1, CONV_DIM), dtb=pad_h(dt_bias), alog=pad_h(a_log),
                dsk=jnp.repeat(d_skip, SSM_HEAD_DIM).reshape(1, D_INNER), ng=ssm_norm_g.reshape(1, D_INNER),
                expand=expand)


def _ssd_prompt(proj, sc):
    t = proj.shape[0]
    q = SSD_CHUNK
    assert t % q == 0
    tri = jnp.asarray(np.tril(np.ones((q, q), np.float32)))
    row = lambda w, cb: pl.BlockSpec((q, w), lambda i, cb=cb: (i, cb))
    const = lambda r, w: pl.BlockSpec((r, w), lambda i: (0, 0))
    return pl.pallas_call(
        _ssd_prompt_kernel, grid=(t // q,),
        in_specs=[row(CONV_DIM, C_XBC // CONV_DIM), row(D_INNER, C_Z // D_INNER), row(MISC_W, C_MISC // MISC_W),
                  const(CONV_W, CONV_DIM), const(1, CONV_DIM), const(1, LANES), const(1, LANES),
                  const(1, D_INNER), const(1, D_INNER), const(LANES, D_INNER), const(q, q)],
        out_specs=[pl.BlockSpec((q, D_INNER), lambda i: (i, 0)),
                   pl.BlockSpec((SSM_HEADS, SSM_HEAD_DIM, D_STATE), lambda i: (0, 0, 0))],
        out_shape=[jax.ShapeDtypeStruct((t, D_INNER), BF16),
                   jax.ShapeDtypeStruct((SSM_HEADS, SSM_HEAD_DIM, D_STATE), F32)],
        scratch_shapes=[pltpu.VMEM((q + 2 * SUBLANES, CONV_DIM), F32), pltpu.VMEM((q, CONV_DIM), F32),
                        pltpu.VMEM((q, D_INNER), BF16), pltpu.VMEM((q, D_INNER), BF16), pltpu.VMEM((q, D_INNER), BF16),
                        pltpu.VMEM((q, D_INNER), F32), pltpu.VMEM((q, D_INNER), F32),
                        pltpu.VMEM((SSM_HEADS // 2, D_STATE, LANES), F32)],
        compiler_params=_cparams("arbitrary"), name="ssd_prompt",
    )(proj, proj, proj, sc["cw"], sc["cb"], sc["dtb"], sc["alog"], sc["dsk"], sc["ng"], sc["expand"], tri)


def _ssd_sample_kernel(xbc_ref, z_ref, misc_ref, cst_ref, st_ref, cw_ref, cb_ref, dtb_ref, alog_ref, dsk_ref, ng_ref,
                       exp_ref, gsum_ref, gexp_ref, o_ref, nst_ref, ext_ref, xa_ref, y_ref, *, t):
    nt = (((1,), (1,)), ((), ()))
    tn = (((0,), (0,)), ((), ()))
    ext_ref[...] = jnp.zeros_like(ext_ref)
    ext_ref[0:CONV_W - 1, :] = cst_ref[0]
    ext_ref[CONV_W - 1:CONV_W - 1 + t, :] = xbc_ref[0]
    _conv_silu(ext_ref, 0, t, cw_ref, cb_ref, xa_ref)

    dt = _softplus(misc_ref[0][:, MISC_DT:MISC_DT + LANES] + dtb_ref[...])
    da = dt * (-jnp.exp(alog_ref[...]))
    rows = [da[0:1]]
    for i in range(1, t):
        rows.append(rows[-1] + da[i:i + 1])
    cs = jnp.concatenate(rows, axis=0)
    ecs = jnp.exp(cs)
    dec_end = jnp.exp(cs[t - 1:t] - cs)
    parts = _split3(jnp.concatenate([dt, dt * dec_end, ecs, cs], axis=0))

    pairs = [(l, s) for l in range(t) for s in range(l + 1)]
    n_b = SSM_GROUPS * D_STATE
    bm = xa_ref[:, D_INNER:D_INNER + n_b]
    cm = xa_ref[:, D_INNER + n_b:D_INNER + 2 * n_b]
    prod = jnp.concatenate([cm[l:l + 1] * bm[s:s + 1] for l, s in pairs], axis=0)
    cb_parts = _split3(_dot3(_split3(prod), gsum_ref[...]))

    heads_per_group = SSM_HEADS // SSM_GROUPS
    for g in range(SSM_GROUPS):
        sl = slice(g * LANE_BLK, (g + 1) * LANE_BLK)
        ex = _dot3(parts, exp_ref[:, sl])
        cbe = _dot3(cb_parts, gexp_ref[:, sl])
        xs = xa_ref[:, sl]
        xdt, xw, ecs_e, cs_e = xs * ex[0:t], xs * ex[t:2 * t], ex[2 * t:3 * t], ex[3 * t:4 * t]
        yrows = []
        for l in range(t):
            acc = None
            for s in range(l + 1):
                pi = pairs.index((l, s))
                coef = cbe[pi:pi + 1] if s == l else cbe[pi:pi + 1] * jnp.exp(cs_e[l:l + 1] - cs_e[s:s + 1])
                term = coef * xdt[s:s + 1]
                acc = term if acc is None else acc + term
            yrows.append(acc)
        sg = st_ref[0, g * heads_per_group:(g + 1) * heads_per_group].reshape(LANE_BLK, D_STATE)
        cg = xa_ref[:, D_INNER + n_b + g * D_STATE:D_INNER + n_b + (g + 1) * D_STATE].astype(BF16)
        yo = lax.dot_general(cg, sg.astype(BF16), nt, preferred_element_type=F32) * ecs_e
        y_ref[:, sl] = jnp.concatenate(yrows, axis=0) + yo
        bg = xa_ref[:, D_INNER + g * D_STATE:D_INNER + (g + 1) * D_STATE]
        zpad = jnp.zeros((SUBLANES - t, LANE_BLK), F32)
        upd = lax.dot_general(jnp.concatenate([xw, zpad], axis=0).astype(BF16),
                              jnp.concatenate([bg, zpad[:, :D_STATE]], axis=0).astype(BF16), tn,
                              preferred_element_type=F32)
        for r in range(heads_per_group):
            h = g * heads_per_group + r
            nst_ref[0, h] = st_ref[0, h] * ecs[t - 1:t, h:h + 1] + upd[r * SSM_HEAD_DIM:(r + 1) * SSM_HEAD_DIM]

    _gated_norm(y_ref, xa_ref, z_ref, dsk_ref, ng_ref, o_ref, (0,))


def _ssd_sample(proj3, conv_state, ssm_state, sc):
    b, t = proj3.shape[:2]
    assert t + CONV_W - 1 <= SUBLANES
    n_b = SSM_GROUPS * D_STATE
    gsum = jnp.asarray(np.arange(n_b)[:, None] // D_STATE == np.arange(LANES)[None, :], BF16)
    gexp = jnp.asarray(np.arange(LANES)[:, None] == (np.arange(D_INNER) // LANE_BLK)[None, :], BF16)
    row = lambda w, cb: pl.BlockSpec((1, t, w), lambda i, cb=cb: (i, 0, cb))
    const = lambda r, w: pl.BlockSpec((r, w), lambda i: (0, 0))
    st_spec = pl.BlockSpec((1, SSM_HEADS, SSM_HEAD_DIM, D_STATE), lambda i: (i, 0, 0, 0))
    kern = functools.partial(_ssd_sample_kernel, t=t)
    return pl.pallas_call(
        kern, grid=(b,),
        in_specs=[row(CONV_DIM, C_XBC // CONV_DIM), row(D_INNER, C_Z // D_INNER), row(MISC_W, C_MISC // MISC_W),
                  pl.BlockSpec((1, CONV_W - 1, CONV_DIM), lambda i: (i, 0, 0)), st_spec,
                  const(CONV_W, CONV_DIM), const(1, CONV_DIM), const(1, LANES), const(1, LANES),
                  const(1, D_INNER), const(1, D_INNER), const(LANES, D_INNER), const(n_b, LANES),
                  const(LANES, D_INNER)],
        out_specs=[pl.BlockSpec((1, t, D_INNER), lambda i: (i, 0, 0)), st_spec],
        out_shape=[jax.ShapeDtypeStruct((b, t, D_INNER), BF16),
                   jax.ShapeDtypeStruct((b, SSM_HEADS, SSM_HEAD_DIM, D_STATE), F32)],
        scratch_shapes=[pltpu.VMEM((SUBLANES, CONV_DIM), F32), pltpu.VMEM((t, CONV_DIM), F32),
                        pltpu.VMEM((t, D_INNER), F32)],
        compiler_params=_cparams("parallel"), name="ssd_sample",
    )(proj3, proj3, proj3, conv_state, ssm_state, sc["cw"], sc["cb"], sc["dtb"], sc["alog"], sc["dsk"], sc["ng"],
      sc["expand"], gsum, gexp)


SEQS_PER_STEP = 8
IDX_PAGES_PER_STEP = 8
ATTN_SEQS_PER_STEP = 2
Q_SLOTS = SUBLANES
NT_DIMS = (((1,), (1,)), ((), ()))


def _scores_sample_kernel(pt_ref, *refs, pps, past, t_new, topk, idx_bits):
    del pt_ref
    gs = SEQS_PER_STEP
    page_refs = refs[:gs * pps]
    iq_ref, w_ref, ikn_ref, rep_ref, bias_ref, keys_ref = refs[gs * pps:]
    j = pl.program_id(1)
    lp = past + LANES

    def head_sum(d, w):
        s = jnp.maximum(d, 0.0) * w
        return jnp.sum(s.reshape(IDX_HEADS, Q_SLOTS, LANES), axis=0)

    for k in range(gs):
        for p in range(pps):
            d = jnp.dot(iq_ref[k], page_refs[k * pps + p][0].astype(BF16), preferred_element_type=F32)
            col = pl.multiple_of((j * pps + p) * LANES, LANES)
            keys_ref[k, :, pl.ds(col, LANES)] = _sortable_key(head_sum(d, w_ref[k]))

    @pl.when(j == pl.num_programs(1) - 1)
    def _():
        tq = lax.broadcasted_iota(I32, (Q_SLOTS, LANES), 0)
        jn = lax.broadcasted_iota(I32, (Q_SLOTS, LANES), 1)
        ok_new = jnp.logical_and(jn <= tq, jn < t_new)
        for k in range(gs):
            d = lax.dot_general(iq_ref[k], ikn_ref[k], NT_DIMS, preferred_element_type=F32)
            s = head_sum(d, w_ref[k])
            keys_ref[k, :, past:lp] = _sortable_key(jnp.where(ok_new, s, -jnp.inf))

        idx = lax.broadcasted_iota(I32, (gs, Q_SLOTS, lp), 2)
        slot = lax.broadcasted_iota(I32, (gs, Q_SLOTS, lp), 1)
        cnt = lambda hit: jnp.sum(hit.astype(I32), axis=2, keepdims=True)
        count_ge = lambda cand: cnt(keys_ref[...] >= cand)
        count_gt = lambda cand: cnt(keys_ref[...] > cand)
        count_eq_before = lambda tau, cutc: cnt(jnp.where(keys_ref[...] == tau, (idx < cutc).astype(I32), 0))
        live = lax.broadcasted_iota(I32, (gs, Q_SLOTS, 1), 1) < t_new
        tau, cut = _select_threshold(count_ge, count_gt, count_eq_before, topk, idx_bits, live)
        keys = keys_ref[...]
        sel = jnp.where(keys == tau, (idx <= cut).astype(I32), (keys > tau).astype(I32))
        new_ok = jnp.where(idx - past <= slot, (idx - past < t_new).astype(I32), 0)
        valid = jnp.where(idx < past, 1, new_ok)
        chosen = jnp.where(sel * valid > 0, 1.0, 0.0).astype(BF16).reshape(gs * Q_SLOTS, lp)
        wide = LANES * KV_HEADS
        for j in range(lp // LANES):
            rep = jnp.dot(chosen[:, j * LANES:(j + 1) * LANES], rep_ref[...], preferred_element_type=F32)
            bias_ref[:, :, j * wide:(j + 1) * wide] = jnp.where(rep > 0.5, 0.0, NEG_BIG).reshape(gs, Q_SLOTS, wide)


def _scores_sample(page_table, cache_idx_t, iq_rows, w_rows, ikn, t_new, topk):
    b, n_pages = page_table.shape
    page = cache_idx_t.shape[2]
    assert page == LANES and b % SEQS_PER_STEP == 0
    past = n_pages * page
    lp = past + LANES
    gs = SEQS_PER_STEP
    pps = math.gcd(IDX_PAGES_PER_STEP, n_pages)
    page_spec = lambda k, p: pl.BlockSpec((1, IDX_DIM, page),
                                          lambda sg, j, pt, k=k, p=p: (pt[sg * gs + k, j * pps + p], 0, 0))
    grp = lambda r, w: pl.BlockSpec((gs, r, w), lambda sg, j, pt: (sg, 0, 0))
    rows = IDX_HEADS * Q_SLOTS
    repeat_lanes = jnp.asarray(np.arange(LANES)[:, None] == np.arange(LANES * KV_HEADS)[None, :] // KV_HEADS, BF16)
    kern = functools.partial(_scores_sample_kernel, pps=pps, past=past, t_new=t_new, topk=topk,
                             idx_bits=max(1, (lp - 1).bit_length()))
    return pl.pallas_call(
        kern,
        grid_spec=pltpu.PrefetchScalarGridSpec(
            num_scalar_prefetch=1, grid=(b // gs, n_pages // pps),
            in_specs=[page_spec(k, p) for k in range(gs) for p in range(pps)] + [
                grp(rows, IDX_DIM), grp(rows, LANES), grp(LANES, IDX_DIM),
                pl.BlockSpec((LANES, LANES * KV_HEADS), lambda sg, j, pt: (0, 0))],
            out_specs=grp(Q_SLOTS, lp * KV_HEADS),
            scratch_shapes=[pltpu.VMEM((gs, Q_SLOTS, lp), I32)]),
        out_shape=jax.ShapeDtypeStruct((b, Q_SLOTS, lp * KV_HEADS), F32),
        compiler_params=_cparams("parallel", "arbitrary"), name="sample_index_select",
    )(page_table, *([cache_idx_t] * (gs * pps)), iq_rows, w_rows, ikn, repeat_lanes)


def _attn_sample_kernel(pt_ref, *refs, n_pages, sps):
    del pt_ref
    k_refs, v_refs = refs[:sps * n_pages], refs[sps * n_pages:2 * sps * n_pages]
    kn_ref, vn_ref, q_ref, bias_ref, o_ref = refs[2 * sps * n_pages:]
    scale2 = HEAD_DIM ** -0.5 * LOG2E
    rows = KV_HEADS * Q_PER_KV * Q_SLOTS
    cols = LANES * KV_HEADS
    row_head = lax.broadcasted_iota(I32, (rows, cols), 0) // (Q_PER_KV * Q_SLOTS)
    col_head = lax.broadcasted_iota(I32, (rows, cols), 1) % KV_HEADS
    own_head = jnp.where(row_head == col_head, 0.0, NEG_BIG).astype(F32)
    for s in range(sps):
        q = q_ref[s]
        new_page = lambda ref: jnp.concatenate(
            [ref[s], jnp.zeros((cols - ref.shape[1], HEAD_DIM), F32)], axis=0)
        k_pages = [r[...] for r in k_refs[s * n_pages:(s + 1) * n_pages]] + [new_page(kn_ref)]
        v_pages = [r[...] for r in v_refs[s * n_pages:(s + 1) * n_pages]] + [new_page(vn_ref)]
        sc = []
        for j, kp in enumerate(k_pages):
            bias = jnp.concatenate([bias_ref[s, :, j * cols:(j + 1) * cols]] * (rows // Q_SLOTS), axis=0)
            sc.append(lax.dot_general(q, kp.astype(BF16), NT_DIMS, preferred_element_type=F32) + (bias + own_head))
        sm = jnp.concatenate(sc, axis=1)
        m = jnp.max(sm, axis=-1, keepdims=True)
        p = jnp.exp2((sm - m) * scale2)
        l = jnp.sum(p, axis=-1, keepdims=True)
        pb = p.astype(BF16)
        acc = None
        for j, vp in enumerate(v_pages):
            t = jnp.dot(pb[:, j * cols:(j + 1) * cols], vp.astype(BF16), preferred_element_type=F32)
            acc = t if acc is None else acc + t
        o_ref[s] = (acc / l).astype(o_ref.dtype)


def _attn_sample(page_table, cache_k, cache_v, kn, vn, q_rows, bias):
    b, n_pages = page_table.shape
    sps = math.gcd(ATTN_SEQS_PER_STEP, b)
    rows = q_rows.shape[1]
    page_rows = LANES * KV_HEADS
    page_spec = lambda s, k: pl.BlockSpec((page_rows, HEAD_DIM), lambda i, pt, s=s, k=k: (pt[i * sps + s, k], 0))
    pages = [page_spec(s, k) for s in range(sps) for k in range(n_pages)]
    per_step = lambda shape: pl.BlockSpec((sps,) + shape, lambda i, pt: (i,) + (0,) * len(shape))
    kern = functools.partial(_attn_sample_kernel, n_pages=n_pages, sps=sps)
    return pl.pallas_call(
        kern,
        grid_spec=pltpu.PrefetchScalarGridSpec(
            num_scalar_prefetch=1, grid=(b // sps,),
            in_specs=pages * 2 + [per_step(kn.shape[1:]), per_step(vn.shape[1:]),
                                  per_step((rows, HEAD_DIM)), per_step((Q_SLOTS, bias.shape[-1]))],
            out_specs=per_step((rows, HEAD_DIM))),
        out_shape=jax.ShapeDtypeStruct((b, rows, HEAD_DIM), BF16),
        compiler_params=_cparams("parallel"), name="sample_attention",
    )(page_table, *([cache_k] * (sps * n_pages)), *([cache_v] * (sps * n_pages)), kn, vn, q_rows, bias)


def _pad_rows(a, axis, size):
    widths = [(0, 0)] * a.ndim
    widths[axis] = (0, size - a.shape[axis])
    return jnp.pad(a, widths)


def _trunk(x, proj, attn_o, ssm_o, w):
    m = x.shape[0]
    tm = min(1024, m)
    mixed = _mix(attn_o, ssm_o, w["pa"], w["ps"], proj, tm, 512)
    h, hn = _outproj(x, mixed, w["out"], w["norm_mlp_g"], min(512, m))
    return _mlp(hn, h, w["up"], w["down"], min(512, m), 1024)


def kernel(x_prompt, x_sample, cache_k, cache_v, cache_idx_k, state_conv, state_ssm, page_table, norm_mix_g, w_in,
           q_norm_g, k_norm_g, idx_ln_g, idx_ln_b, conv_w, conv_b, dt_bias, a_log, d_skip, ssm_norm_g, w_proj_attn,
           w_proj_ssm, w_out, norm_mlp_g, w_up, w_down):
    depth = w_in.shape[0]
    bp, seq = x_prompt.shape[:2]
    db, t_new = x_sample.shape[:2]
    assert depth == 1 and bp == 1, "single-layer, single-prompt configuration"
    n_pages = page_table.shape[1]
    page = cache_k.shape[2]
    past = n_pages * page
    nkv = KV_HEADS * HEAD_DIM
    l = 0

    w_p = _regroup_w_in(jnp.swapaxes(w_in[l], 0, 1))
    w = dict(pa=w_proj_attn[l].astype(BF16), ps=w_proj_ssm[l].astype(BF16), out=w_out[l].astype(BF16),
             up=w_up[l].astype(BF16), down=w_down[l].astype(BF16), norm_mlp_g=norm_mlp_g[l])
    sc = _ssm_consts(conv_w[l], conv_b[l], dt_bias[l], a_log[l], d_skip[l], ssm_norm_g[l])
    qg, kg = q_norm_g[l].reshape(1, HEAD_DIM), k_norm_g[l].reshape(1, HEAD_DIM)
    lg = _pad_rows(idx_ln_g[l].reshape(1, IDX_DIM), 1, LANES)
    lb = _pad_rows(idx_ln_b[l].reshape(1, IDX_DIM), 1, LANES)

    def front(x, pos):
        m = x.shape[0]
        xn = _rmsnorm(x, norm_mix_g[l], min(512, m))
        proj = _matmul(xn, w_p, min(1024, m), 1024)
        return proj, _prep(proj, _rope_tables(pos), qg, kg, lg, lb)

    xp = x_prompt.reshape(seq, D_MODEL)
    proj_p, pp_ = front(xp, jnp.arange(seq))
    attn_p = _attn_prompt(pp_, min(TOPK_MAX, seq // 4), q_norm_g[l], k_norm_g[l])
    ssm_p, state_p = _ssd_prompt(proj_p, sc)
    y_p = _trunk(xp, proj_p, attn_p, ssm_p, w)

    xs = x_sample.reshape(db * t_new, D_MODEL)
    proj_s, ps_ = front(xs, jnp.tile(past + jnp.arange(t_new), db))
    rh = Q_PER_KV
    q_rows = _pad_rows(ps_["q_bf"].reshape(db, t_new, KV_HEADS, rh, HEAD_DIM).transpose(0, 2, 3, 1, 4), 3, Q_SLOTS)
    q_rows = q_rows.reshape(db, KV_HEADS * rh * Q_SLOTS, HEAD_DIM)
    iq_rows = _pad_rows(ps_["iq_bf"].reshape(db, t_new, IDX_HEADS, IDX_DIM).transpose(0, 2, 1, 3), 2, Q_SLOTS)
    iq_rows = iq_rows.reshape(db, IDX_HEADS * Q_SLOTS, IDX_DIM)
    w_rows = _pad_rows(ps_["wT"].reshape(IDX_HEADS, db, t_new).transpose(1, 0, 2), 2, Q_SLOTS)
    w_rows = jnp.broadcast_to(w_rows.reshape(db, IDX_HEADS * Q_SLOTS, 1), (db, IDX_HEADS * Q_SLOTS, LANES))
    ikn = _pad_rows(ps_["ik_bf"].reshape(db, t_new, IDX_DIM), 1, LANES)
    kn = ps_["k_f32"].reshape(db, t_new * KV_HEADS, HEAD_DIM)
    v_s = proj_s[:, C_V:C_V + nkv]
    vn = v_s.reshape(db, t_new * KV_HEADS, HEAD_DIM)
    bias = _scores_sample(page_table, jnp.swapaxes(cache_idx_k[l], 1, 2), iq_rows, w_rows, ikn, t_new,
                          min(TOPK_MAX, (past + t_new) // 4))
    n_pool = cache_k.shape[1]
    assert page == LANES
    o_rows = _attn_sample(page_table, cache_k[l].reshape(n_pool * page * KV_HEADS, HEAD_DIM),
                          cache_v[l].reshape(n_pool * page * KV_HEADS, HEAD_DIM), kn, vn, q_rows, bias)
    attn_s = o_rows.reshape(db, KV_HEADS, rh, Q_SLOTS, HEAD_DIM)[:, :, :, :t_new].transpose(0, 3, 1, 2, 4)
    attn_s = attn_s.reshape(db * t_new, N_HEADS * HEAD_DIM)
    ssm_s, state_s = _ssd_sample(proj_s.reshape(db, t_new, N_PROJ), state_conv[l], state_ssm[l], sc)
    y_s = _trunk(xs, proj_s, attn_s, ssm_s.reshape(db * t_new, D_INNER), w)

    kv_shape = lambda b, t: (1, b, t, KV_HEADS, HEAD_DIM)
    xbc_p = proj_p[seq - (CONV_W - 1):, C_XBC:C_XBC + CONV_DIM]
    xbc_s = proj_s.reshape(db, t_new, N_PROJ)[:, :, C_XBC:C_XBC + CONV_DIM]
    conv_s = jnp.concatenate([state_conv[l], xbc_s], axis=1)[:, t_new:]
    return (y_p.reshape(bp, seq, D_MODEL), y_s.reshape(db, t_new, D_MODEL),
            pp_["k_f32"].reshape(kv_shape(bp, seq)), proj_p[:, C_V:C_V + nkv].reshape(kv_shape(bp, seq)),
            pp_["ik_f32"].reshape(1, bp, seq, IDX_DIM), xbc_p.reshape(1, bp, CONV_W - 1, CONV_DIM),
            state_p.reshape(1, bp, SSM_HEADS, SSM_HEAD_DIM, D_STATE),
            ps_["k_f32"].reshape(kv_shape(db, t_new)), v_s.reshape(kv_shape(db, t_new)),
            ps_["ik_f32"].reshape(1, db, t_new, IDX_DIM), conv_s.reshape(1, db, CONV_W - 1, CONV_DIM),
            state_s.reshape(1, db, SSM_HEADS, SSM_HEAD_DIM, D_STATE))
```

```python
import functools
import math

import jax
import jax.numpy as jnp
import numpy as np
from jax import lax
from jax.experimental import pallas as pl
from jax.experimental.pallas import tpu as pltpu

F32, BF16, I32 = jnp.float32, jnp.bfloat16, jnp.int32

D_MODEL = 2048
N_HEADS = 16
KV_HEADS = 4
HEAD_DIM = 128
Q_PER_KV = N_HEADS // KV_HEADS
IDX_HEADS = 16
IDX_DIM = 64
TOPK_MAX = 256
D_INNER = 4096
SSM_HEAD_DIM = 64
SSM_HEADS = 64
SSM_GROUPS = 8
D_STATE = 128
CONV_W = 4
CONV_DIM = D_INNER + 2 * SSM_GROUPS * D_STATE
SSD_CHUNK = 128
D_FF = 4 * D_MODEL
ROPE_THETA = 10000.0
EPS = 1e-6
SPLITS = (N_HEADS * HEAD_DIM, KV_HEADS * HEAD_DIM, KV_HEADS * HEAD_DIM, IDX_HEADS * IDX_DIM, IDX_DIM, IDX_HEADS,
          D_INNER, CONV_DIM, SSM_HEADS, D_MODEL, D_MODEL)

LANES = 128
SUBLANES = 8
VMEM_LIMIT = 56 * 1024 * 1024

C_Z = 0
C_Q = 4096
C_XBC = 6144
C_GA = 12288
C_GB = 14336
C_K = 16384
C_V = 16896
C_IQ = 17408
C_MISC = 18432
MISC_DT = 256
REGROUP_BLK = 256
MISC_W = 512
N_PROJ = 19456

INT_MIN = -(2 ** 31)
INT_MAX = 2 ** 31 - 1
NEG_INF_KEY = -2139095041
NEG_BIG = -1e30
Q_TILE = 128
KEY_CHUNK = 1024
COUNT_CHUNK = 512
LOG2E = 1.4426950408889634


def _cparams(*sem):
    return pltpu.CompilerParams(dimension_semantics=sem, vmem_limit_bytes=VMEM_LIMIT)


def _sortable_key(x):
    bits = pltpu.bitcast(x, I32)
    return bits ^ ((bits >> 31) & INT_MAX)


def _rmsnorm_kernel(x_ref, g_ref, o_ref):
    x = x_ref[...]
    ms = jnp.mean(x * x, axis=-1, keepdims=True)
    o_ref[...] = (x * lax.rsqrt(ms + EPS) * g_ref[...]).astype(o_ref.dtype)


def _rmsnorm(x, g, tm):
    m, d = x.shape
    return pl.pallas_call(
        _rmsnorm_kernel, grid=(m // tm,),
        in_specs=[pl.BlockSpec((tm, d), lambda i: (i, 0)), pl.BlockSpec((1, d), lambda i: (0, 0))],
        out_specs=pl.BlockSpec((tm, d), lambda i: (i, 0)),
        out_shape=jax.ShapeDtypeStruct((m, d), BF16),
        compiler_params=_cparams("parallel"), name="rmsnorm",
    )(x, g.reshape(1, d))


def _matmul_kernel(a_ref, w_ref, o_ref):
    o_ref[...] = jnp.dot(a_ref[...], w_ref[...], preferred_element_type=F32)


def _matmul(a, w, tm, tn):
    m, k = a.shape
    n = w.shape[1]
    return pl.pallas_call(
        _matmul_kernel, grid=(n // tn, m // tm),
        in_specs=[pl.BlockSpec((tm, k), lambda j, i: (i, 0)), pl.BlockSpec((k, tn), lambda j, i: (0, j))],
        out_specs=pl.BlockSpec((tm, tn), lambda j, i: (i, j)),
        out_shape=jax.ShapeDtypeStruct((m, n), F32),
        compiler_params=_cparams("parallel", "parallel"), name="in_proj",
    )(a, w)


def _regroup_plan():
    src = dict(zip(("q", "k", "v", "iq", "ik", "iw", "z", "xbc", "dt", "ga", "gb"),
                   [int(c) for c in np.cumsum((0,) + SPLITS[:-1])]))
    segs = [(C_Z, src["z"], D_INNER), (C_Q, src["q"], N_HEADS * HEAD_DIM), (C_XBC, src["xbc"], CONV_DIM),
            (C_GA, src["ga"], D_MODEL), (C_GB, src["gb"], D_MODEL), (C_K, src["k"], KV_HEADS * HEAD_DIM),
            (C_V, src["v"], KV_HEADS * HEAD_DIM), (C_IQ, src["iq"], IDX_HEADS * IDX_DIM),
            (C_MISC, src["ik"], IDX_DIM + IDX_HEADS), (C_MISC + MISC_DT, src["dt"], SSM_HEADS)]
    start = np.zeros((N_PROJ // REGROUP_BLK,), np.int32)
    valid = np.zeros_like(start)
    for dst, s0, n in segs:
        assert dst % REGROUP_BLK == 0
        for off in range(0, n, REGROUP_BLK):
            j = (dst + off) // REGROUP_BLK
            start[j], valid[j] = s0 + off, min(REGROUP_BLK, n - off)
    return start, valid


def _regroup_kernel(start_ref, valid_ref, a_ref, o_ref):
    del start_ref
    row = lax.broadcasted_iota(I32, a_ref.shape, 0)
    x = jnp.where(row < valid_ref[pl.program_id(0)], a_ref[...], 0.0)
    o_ref[...] = x.T.astype(o_ref.dtype)


def _regroup_w_in(w_t):
    n_src, k = w_t.shape
    start, valid = _regroup_plan()
    assert int(start.max()) + REGROUP_BLK <= n_src and all(int(s) % SUBLANES == 0 for s in start)
    start = start // SUBLANES
    return pl.pallas_call(
        _regroup_kernel,
        grid_spec=pltpu.PrefetchScalarGridSpec(
            num_scalar_prefetch=2, grid=(N_PROJ // REGROUP_BLK,),
            in_specs=[pl.BlockSpec((pl.Element(REGROUP_BLK), pl.Element(k)),
                                   lambda j, st, va: (st[j] * SUBLANES, 0))],
            out_specs=pl.BlockSpec((k, REGROUP_BLK), lambda j, st, va: (0, j))),
        out_shape=jax.ShapeDtypeStruct((k, N_PROJ), BF16),
        compiler_params=_cparams("parallel"), name="regroup_w_in",
    )(jnp.asarray(start), jnp.asarray(valid), w_t)


def _mix_kernel(a_ref, s_ref, wa_ref, ws_ref, ga_ref, gb_ref, o_ref):
    pa = jnp.dot(a_ref[...], wa_ref[...], preferred_element_type=F32)
    ps = jnp.dot(s_ref[...], ws_ref[...], preferred_element_type=F32)
    o_ref[...] = (jax.nn.sigmoid(ga_ref[...]) * pa + jax.nn.sigmoid(gb_ref[...]) * ps).astype(o_ref.dtype)


def _mix(attn_o, ssm_o, wa, ws, proj, tm, tn):
    m = attn_o.shape[0]
    return pl.pallas_call(
        _mix_kernel, grid=(D_MODEL // tn, m // tm),
        in_specs=[pl.BlockSpec((tm, N_HEADS * HEAD_DIM), lambda j, i: (i, 0)),
                  pl.BlockSpec((tm, D_INNER), lambda j, i: (i, 0)),
                  pl.BlockSpec((N_HEADS * HEAD_DIM, tn), lambda j, i: (0, j)),
                  pl.BlockSpec((D_INNER, tn), lambda j, i: (0, j)),
                  pl.BlockSpec((tm, tn), lambda j, i: (i, C_GA // tn + j)),
                  pl.BlockSpec((tm, tn), lambda j, i: (i, C_GB // tn + j))],
        out_specs=pl.BlockSpec((tm, tn), lambda j, i: (i, j)),
        out_shape=jax.ShapeDtypeStruct((m, D_MODEL), BF16),
        compiler_params=_cparams("parallel", "parallel"), name="gated_merge",
    )(attn_o, ssm_o, wa, ws, proj, proj)


def _outproj_kernel(x_ref, mix_ref, w_ref, g_ref, h_ref, hn_ref):
    h = x_ref[...] + jnp.dot(mix_ref[...], w_ref[...], preferred_element_type=F32)
    h_ref[...] = h
    ms = jnp.mean(h * h, axis=-1, keepdims=True)
    hn_ref[...] = (h * lax.rsqrt(ms + EPS) * g_ref[...]).astype(hn_ref.dtype)


def _outproj(x, mixed, w, g, tm):
    m = x.shape[0]
    return pl.pallas_call(
        _outproj_kernel, grid=(m // tm,),
        in_specs=[pl.BlockSpec((tm, D_MODEL), lambda i: (i, 0)), pl.BlockSpec((tm, D_MODEL), lambda i: (i, 0)),
                  pl.BlockSpec((D_MODEL, D_MODEL), lambda i: (0, 0)), pl.BlockSpec((1, D_MODEL), lambda i: (0, 0))],
        out_specs=[pl.BlockSpec((tm, D_MODEL), lambda i: (i, 0)), pl.BlockSpec((tm, D_MODEL), lambda i: (i, 0))],
        out_shape=[jax.ShapeDtypeStruct((m, D_MODEL), F32), jax.ShapeDtypeStruct((m, D_MODEL), BF16)],
        compiler_params=_cparams("parallel"), name="out_proj",
    )(x, mixed, w, g.reshape(1, D_MODEL))


def _mlp_kernel(hn_ref, h_ref, wu_ref, wd_ref, o_ref):
    f = pl.program_id(1)
    u = jnp.dot(hn_ref[...], wu_ref[...], preferred_element_type=F32)
    act = jnp.square(jnp.maximum(u, 0.0)).astype(BF16)
    contrib = jnp.dot(act, wd_ref[...], preferred_element_type=F32)

    @pl.when(f == 0)
    def _():
        o_ref[...] = h_ref[...] + contrib

    @pl.when(f > 0)
    def _():
        o_ref[...] += contrib


def _mlp(hn, h, wu, wd, tm, tf):
    m = h.shape[0]
    return pl.pallas_call(
        _mlp_kernel, grid=(m // tm, D_FF // tf),
        in_specs=[pl.BlockSpec((tm, D_MODEL), lambda i, f: (i, 0)), pl.BlockSpec((tm, D_MODEL), lambda i, f: (i, 0)),
                  pl.BlockSpec((D_MODEL, tf), lambda i, f: (0, f)), pl.BlockSpec((tf, D_MODEL), lambda i, f: (f, 0))],
        out_specs=pl.BlockSpec((tm, D_MODEL), lambda i, f: (i, 0)),
        out_shape=jax.ShapeDtypeStruct((m, D_MODEL), F32),
        compiler_params=_cparams("parallel", "arbitrary"), name="mlp",
    )(hn, h, wu, wd)


def _prep_kernel(q_ref, k_ref, v_ref, iq_ref, misc_ref, ca_ref, sa_ref, cb_ref, sb_ref, qg_ref, kg_ref, lg_ref,
                 lb_ref, kf_ref, kb_ref, ikf_ref, ikb_ref, qn_ref, iqn_ref, qt_ref, iqt_ref, wt_ref, vt_ref):
    ca, sa, cb, sb = ca_ref[...], sa_ref[...], cb_ref[...], sb_ref[...]
    lane = lax.broadcasted_iota(I32, (Q_TILE, LANES), 1)
    first_half = (lane % IDX_DIM) < (IDX_DIM // 2)

    def norm_rope(x, g):
        ms = jnp.mean(x * x, axis=-1, keepdims=True)
        xn = x * lax.rsqrt(ms + EPS) * g
        return xn * ca + pltpu.roll(xn, HEAD_DIM // 2, 1) * sa

    def rope_idx(x):
        rot = jnp.where(first_half, pltpu.roll(x, LANES - IDX_DIM // 2, 1), pltpu.roll(x, IDX_DIM // 2, 1))
        return x * cb + rot * sb

    for h in range(N_HEADS):
        sl = slice(h * HEAD_DIM, (h + 1) * HEAD_DIM)
        qh = norm_rope(q_ref[:, sl], qg_ref[...])
        qn_ref[:, sl] = qh.astype(BF16)
        qt_ref[sl, :] = qh.T.astype(BF16)
    for h in range(KV_HEADS):
        sl = slice(h * HEAD_DIM, (h + 1) * HEAD_DIM)
        kh = norm_rope(k_ref[:, sl], kg_ref[...])
        kf_ref[:, sl] = kh
        kb_ref[:, sl] = kh.astype(BF16)
        vt_ref[sl, :] = v_ref[:, sl].T.astype(BF16)
    for c in range(IDX_HEADS * IDX_DIM // LANES):
        sl = slice(c * LANES, (c + 1) * LANES)
        x = rope_idx(iq_ref[:, sl])
        iqn_ref[:, sl] = x.astype(BF16)
        iqt_ref[sl, :] = x.T.astype(BF16)

    m = misc_ref[:, 0:LANES]
    is_k = lane < IDX_DIM
    mu = jnp.sum(jnp.where(is_k, m, 0.0), axis=-1, keepdims=True) * (1.0 / IDX_DIM)
    xc = jnp.where(is_k, m - mu, 0.0)
    var = jnp.sum(xc * xc, axis=-1, keepdims=True) * (1.0 / IDX_DIM)
    y = rope_idx(xc * lax.rsqrt(var + EPS) * lg_ref[...] + lb_ref[...])
    ikf_ref[...] = y[:, :IDX_DIM]
    ikb_ref[...] = y[:, :IDX_DIM].astype(BF16)
    wt_ref[...] = m.T[IDX_DIM:IDX_DIM + IDX_HEADS, :] * (IDX_HEADS ** -0.5 * IDX_DIM ** -0.5)


def _prep(proj, tabs, qg, kg, lg, lb):
    t = proj.shape[0]
    tq = Q_TILE
    row = lambda w, cb: pl.BlockSpec((tq, w), lambda i, cb=cb: (i, cb))
    const = lambda w: pl.BlockSpec((1, w), lambda i: (0, 0))
    colT = lambda r: pl.BlockSpec((r, tq), lambda i: (0, i))
    nq, nkv, niq = N_HEADS * HEAD_DIM, KV_HEADS * HEAD_DIM, IDX_HEADS * IDX_DIM
    outs = pl.pallas_call(
        _prep_kernel, grid=(t // tq,),
        in_specs=[row(nq, C_Q // nq), row(nkv, C_K // nkv), row(nkv, C_V // nkv), row(niq, C_IQ // niq),
                  row(MISC_W, C_MISC // MISC_W), row(LANES, 0), row(LANES, 0), row(LANES, 0), row(LANES, 0),
                  const(LANES), const(LANES), const(LANES), const(LANES)],
        out_specs=[row(nkv, 0), row(nkv, 0), row(IDX_DIM, 0), row(IDX_DIM, 0), row(nq, 0), row(niq, 0),
                   colT(nq), colT(niq), colT(IDX_HEADS), colT(nkv)],
        out_shape=[jax.ShapeDtypeStruct((t, nkv), F32), jax.ShapeDtypeStruct((t, nkv), BF16),
                   jax.ShapeDtypeStruct((t, IDX_DIM), F32), jax.ShapeDtypeStruct((t, IDX_DIM), BF16),
                   jax.ShapeDtypeStruct((t, nq), BF16), jax.ShapeDtypeStruct((t, niq), BF16),
                   jax.ShapeDtypeStruct((nq, t), BF16), jax.ShapeDtypeStruct((niq, t), BF16),
                   jax.ShapeDtypeStruct((IDX_HEADS, t), F32), jax.ShapeDtypeStruct((nkv, t), BF16)],
        compiler_params=_cparams("parallel"), name="norm_rope_prep",
    )(proj, proj, proj, proj, proj, *tabs, qg, kg, lg, lb)
    return dict(zip(("k_f32", "k_bf", "ik_f32", "ik_bf", "q_bf", "iq_bf", "qT", "iqT", "wT", "vT"), outs))


def _rope_tables(pos):
    def tab(dim):
        half = dim // 2
        inv = ROPE_THETA ** (-jnp.arange(half, dtype=F32) / half)
        ang = pos.astype(F32)[:, None] * inv[None, :]
        c, s = jnp.cos(ang), jnp.sin(ang)
        rep = LANES // dim
        return jnp.tile(jnp.concatenate([c, c], -1), (1, rep)), jnp.tile(jnp.concatenate([-s, s], -1), (1, rep))
    ca, sa = tab(HEAD_DIM)
    cb, sb = tab(IDX_DIM)
    return ca, sa, cb, sb


def _select_threshold(count_ge, count_gt, count_eq_before, topk, idx_bits, live):
    tau = jnp.full(live.shape, INT_MIN, I32)
    cand = jnp.zeros(live.shape, I32)
    tau = jnp.where(count_ge(cand) >= topk, cand, tau)

    def bit_step(b, tau):
        cand = tau | lax.shift_left(jnp.int32(1), 30 - b)
        return jnp.where(count_ge(cand) >= topk, cand, tau)

    tau = lax.fori_loop(0, 31, bit_step, tau)
    n_gt = count_gt(tau)
    n_eq = count_ge(tau) - n_gt
    need = topk - n_gt
    tie = jnp.logical_and(jnp.logical_and(n_eq > need, tau != NEG_INF_KEY), live)
    has_tie = jnp.max(tie.astype(I32)) > 0

    def with_ties():
        def step(b, cut):
            cand = cut | lax.shift_left(jnp.int32(1), idx_bits - 1 - b)
            return jnp.where(count_eq_before(tau, cand) < need, cand, cut)
        return lax.fori_loop(0, idx_bits, step, jnp.zeros(live.shape, I32))

    cut = lax.cond(has_tie, with_ties, lambda: jnp.full(live.shape, INT_MAX, I32))
    return tau, cut


def _attn_prompt_kernel(bound_ref, ik_ref, k_ref, vt_ref, iqt_ref, wt_ref, qt_ref, o_ref, keys_ref, acc_ref, *, topk,
                        idx_bits, bounded):
    i = pl.program_id(0)
    ch = KEY_CHUNK
    n_ch = (i * Q_TILE + Q_TILE + ch - 1) // ch
    t_glob = i * Q_TILE + lax.broadcasted_iota(I32, (1, Q_TILE), 1)
    row_iota = lax.broadcasted_iota(I32, (ch, Q_TILE), 0)

    def score_chunk(c, carry):
        base = pl.multiple_of(c * ch, ch)
        ikc = ik_ref[pl.ds(base, ch), :]
        acc = jnp.zeros((ch, Q_TILE), F32)
        for h in range(0, IDX_HEADS, 2):
            pair = jnp.concatenate([iqt_ref[h * IDX_DIM:(h + 1) * IDX_DIM, :],
                                    iqt_ref[(h + 1) * IDX_DIM:(h + 2) * IDX_DIM, :]], axis=1)
            d = jnp.dot(ikc, pair, preferred_element_type=F32)
            acc = acc + jnp.maximum(d[:, :Q_TILE], 0.0) * wt_ref[h:h + 1, :]
            acc = acc + jnp.maximum(d[:, Q_TILE:], 0.0) * wt_ref[h + 1:h + 2, :]
        acc = jnp.where(base + row_iota <= t_glob, acc, -jnp.inf)
        keys_ref[pl.ds(base, ch), :] = _sortable_key(acc)
        return carry

    lax.fori_loop(0, n_ch, score_chunk, 0)

    def _count(pred):
        cnt_iota = lax.broadcasted_iota(I32, (COUNT_CHUNK, Q_TILE), 0)

        def body(c, cnt):
            for u in range(ch // COUNT_CHUNK):
                base = pl.multiple_of(c * ch + u * COUNT_CHUNK, COUNT_CHUNK)
                hit = pred(keys_ref[pl.ds(base, COUNT_CHUNK), :], base + cnt_iota).astype(I32)
                cnt = cnt + jnp.sum(hit.reshape(COUNT_CHUNK // SUBLANES, SUBLANES, Q_TILE), axis=0)
            return cnt

        cnt = lax.fori_loop(0, n_ch, body, jnp.zeros((SUBLANES, Q_TILE), I32))
        return jnp.sum(cnt, axis=0, keepdims=True)

    tau, cut = _select_threshold(
        lambda cand: _count(lambda blk, idx: blk >= cand), lambda cand: _count(lambda blk, idx: blk > cand),
        lambda tau, cut: _count(lambda blk, idx: jnp.where(blk == tau, (idx < cut).astype(I32), 0)),
        topk, idx_bits, jnp.ones((1, Q_TILE), jnp.bool_))

    offset = -bound_ref[0] if bounded else 0.0

    def bias_chunk(c, carry):
        base = pl.multiple_of(c * ch, ch)
        blk = keys_ref[pl.ds(base, ch), :]
        idx = base + row_iota
        sel = jnp.where(blk == tau, (idx <= cut).astype(I32), (blk > tau).astype(I32))
        sel = jnp.where(idx <= t_glob, sel, 0)
        keys_ref[pl.ds(base, ch), :] = pltpu.bitcast(jnp.where(sel > 0, offset, NEG_BIG).astype(F32), I32)
        return carry

    lax.fori_loop(0, n_ch, bias_chunk, 0)

    scale2 = HEAD_DIM ** -0.5 * LOG2E
    qgs = [jnp.concatenate([qt_ref[(g * Q_PER_KV + r) * HEAD_DIM:(g * Q_PER_KV + r + 1) * HEAD_DIM, :]
                            for r in range(Q_PER_KV)], axis=1) for g in range(KV_HEADS)]
    acc_ref[...] = jnp.zeros_like(acc_ref)

    def attn_chunk(c, carry):
        base = pl.multiple_of(c * ch, ch)
        bias = jnp.concatenate([pltpu.bitcast(keys_ref[pl.ds(base, ch), :], F32)] * Q_PER_KV, axis=1)
        out = []
        for g in range(KV_HEADS):
            kc = k_ref[pl.ds(base, ch), g * HEAD_DIM:(g + 1) * HEAD_DIM]
            v_t = vt_ref[g * HEAD_DIM:(g + 1) * HEAD_DIM, pl.ds(base, ch)]
            s = jnp.dot(kc, qgs[g], preferred_element_type=F32) + bias
            if bounded:
                p = jnp.exp2(s * scale2)
                acc_ref[g] += jnp.dot(v_t, p.astype(BF16), preferred_element_type=F32)
                out.append(carry[g] + jnp.sum(p, axis=0, keepdims=True))
            else:
                m, l = carry[g]
                m_new = jnp.maximum(m, jnp.max(s, axis=0, keepdims=True))
                p = jnp.exp2((s - m_new) * scale2)
                alpha = jnp.exp2((m - m_new) * scale2)
                acc_ref[g] = acc_ref[g] * alpha + jnp.dot(v_t, p.astype(BF16), preferred_element_type=F32)
                out.append((m_new, l * alpha + jnp.sum(p, axis=0, keepdims=True)))
        return tuple(out)

    l0 = jnp.zeros((1, Q_PER_KV * Q_TILE), F32)
    init = l0 if bounded else (jnp.full((1, Q_PER_KV * Q_TILE), NEG_BIG, F32), l0)
    fin = lax.fori_loop(0, n_ch, attn_chunk, (init,) * KV_HEADS)
    for g in range(KV_HEADS):
        o = acc_ref[g] / (fin[g] if bounded else fin[g][1])
        for r in range(Q_PER_KV):
            h = g * Q_PER_KV + r
            o_ref[:, h * HEAD_DIM:(h + 1) * HEAD_DIM] = o[:, r * Q_TILE:(r + 1) * Q_TILE].T.astype(o_ref.dtype)


MAX_BOUNDED_LOGIT = 40.0


def _attn_prompt_call(p, topk, bound, bounded):
    t = p["k_bf"].shape[0]
    assert t % KEY_CHUNK == 0, "prompt length must be a multiple of the key chunk"
    nq, nkv, niq = N_HEADS * HEAD_DIM, KV_HEADS * HEAD_DIM, IDX_HEADS * IDX_DIM
    full = lambda shape: pl.BlockSpec(shape, lambda i: (0, 0))
    colT = lambda r: pl.BlockSpec((r, Q_TILE), lambda i: (0, i))
    kern = functools.partial(_attn_prompt_kernel, topk=topk, idx_bits=max(1, (t - 1).bit_length()), bounded=bounded)
    return pl.pallas_call(
        kern, grid=(t // Q_TILE,),
        in_specs=[pl.BlockSpec(memory_space=pltpu.SMEM), full((t, IDX_DIM)), full((t, nkv)), full((nkv, t)),
                  colT(niq), colT(IDX_HEADS), colT(nq)],
        out_specs=pl.BlockSpec((Q_TILE, nq), lambda i: (i, 0)),
        out_shape=jax.ShapeDtypeStruct((t, nq), BF16),
        scratch_shapes=[pltpu.VMEM((t, Q_TILE), I32), pltpu.VMEM((KV_HEADS, HEAD_DIM, Q_PER_KV * Q_TILE), F32)],
        compiler_params=_cparams("arbitrary"), name="prompt_attention_bounded" if bounded else "prompt_attention",
    )(bound, p["ik_bf"], p["k_bf"], p["vT"], p["iqT"], p["wT"], p["qT"])


def _attn_prompt(p, topk, q_gain, k_gain):
    bound = (HEAD_DIM * 1.02) * jnp.max(jnp.abs(q_gain)) * jnp.max(jnp.abs(k_gain))
    bound = bound.reshape(1).astype(F32)
    return lax.cond(bound[0] * HEAD_DIM ** -0.5 <= MAX_BOUNDED_LOGIT,
                    lambda: _attn_prompt_call(p, topk, bound, True),
                    lambda: _attn_prompt_call(p, topk, bound, False))


def _softplus(x):
    return jnp.maximum(x, 0.0) + jnp.log1p(jnp.exp(-jnp.abs(x)))


def _split3(v):
    v1 = v.astype(BF16)
    r1 = v - v1.astype(F32)
    v2 = r1.astype(BF16)
    v3 = (r1 - v2.astype(F32)).astype(BF16)
    return v1, v2, v3


def _dot3(parts, e):
    return sum(jnp.dot(p, e, preferred_element_type=F32) for p in parts)


LANE_BLK = 512


def _conv_silu(ext_ref, row0, rows, cw_ref, cb_ref, xa_ref):
    for j in range(CONV_DIM // LANE_BLK):
        sl = slice(j * LANE_BLK, (j + 1) * LANE_BLK)
        acc = ext_ref[row0:row0 + rows, sl] * cw_ref[0:1, sl]
        for i in range(1, CONV_W):
            acc = acc + ext_ref[row0 + i:row0 + i + rows, sl] * cw_ref[i:i + 1, sl]
        acc = acc + cb_ref[:, sl]
        xa_ref[:, sl] = acc * jax.nn.sigmoid(acc)


def _gated_norm(y_ref, xa_ref, z_ref, dsk_ref, ng_ref, o_ref, zidx):
    for j in range(SSM_GROUPS):
        sl = slice(j * LANE_BLK, (j + 1) * LANE_BLK)
        y = y_ref[:, sl] + dsk_ref[:, sl] * xa_ref[:, sl]
        zz = z_ref[zidx + (slice(None), sl)] if zidx else z_ref[:, sl]
        y = y * (zz * jax.nn.sigmoid(zz))
        ms = jnp.mean(y * y, axis=-1, keepdims=True)
        res = (y * lax.rsqrt(ms + EPS) * ng_ref[:, sl]).astype(o_ref.dtype)
        if zidx:
            o_ref[zidx + (slice(None), sl)] = res
        else:
            o_ref[:, sl] = res


def _ssd_prompt_kernel(xbc_ref, z_ref, misc_ref, cw_ref, cb_ref, dtb_ref, alog_ref, dsk_ref, ng_ref, exp_ref, tri_ref,
                       o_ref, st_ref, ext_ref, xa_ref, xlo_ref, xhi_ref, xw_ref, ecs_ref, y_ref, state_ref):
    c = pl.program_id(0)
    q = SSD_CHUNK
    pad = SUBLANES

    @pl.when(c == 0)
    def _():
        ext_ref[0:pad, :] = jnp.zeros((pad, CONV_DIM), F32)
        state_ref[...] = jnp.zeros_like(state_ref)

    ext_ref[pad:pad + q, :] = xbc_ref[...]
    _conv_silu(ext_ref, pad - (CONV_W - 1), q, cw_ref, cb_ref, xa_ref)
    ext_ref[0:pad, :] = ext_ref[q:q + pad, :]

    dt = _softplus(misc_ref[:, MISC_DT:MISC_DT + LANES] + dtb_ref[...])
    da = dt * (-jnp.exp(alog_ref[...]))
    cs = jnp.dot(tri_ref[...], da, precision=lax.Precision.HIGHEST, preferred_element_type=F32)
    cs_t = cs.T
    ecs = jnp.exp(cs)
    dec_end = jnp.exp(cs[q - 1:q, :] - cs)
    parts = _split3(jnp.concatenate([dt, dt * dec_end, ecs], axis=0))
    lane = lax.broadcasted_iota(I32, (q, LANE_BLK), 1)
    lo = (lane % LANES) < SSM_HEAD_DIM
    for j in range(D_INNER // LANE_BLK):
        sl = slice(j * LANE_BLK, (j + 1) * LANE_BLK)
        ex = _dot3(parts, exp_ref[:, sl])
        xs = xa_ref[:, sl]
        xdt = xs * ex[0:q]
        xlo_ref[:, sl] = jnp.where(lo, xdt, 0.0).astype(BF16)
        xhi_ref[:, sl] = jnp.where(lo, 0.0, xdt).astype(BF16)
        xw_ref[:, sl] = (xs * ex[q:2 * q]).astype(BF16)
        ecs_ref[:, sl] = ex[2 * q:3 * q]

    causal = lax.broadcasted_iota(I32, (q, q), 0) >= lax.broadcasted_iota(I32, (q, q), 1)
    heads_per_group = SSM_HEADS // SSM_GROUPS
    for g in range(SSM_GROUPS):
        b0 = D_INNER + g * D_STATE
        c0 = D_INNER + SSM_GROUPS * D_STATE + g * D_STATE
        bg = xa_ref[:, b0:b0 + D_STATE]
        cg = xa_ref[:, c0:c0 + D_STATE].astype(BF16)
        cbm = lax.dot_general(cg, bg.astype(BF16), (((1,), (1,)), ((), ())), preferred_element_type=F32)
        bg_t = bg.T.astype(BF16)
        for j in range(heads_per_group // 2):
            pidx = g * (heads_per_group // 2) + j
            sl = slice(pidx * LANES, (pidx + 1) * LANES)
            yd = None
            for hh, x_ref in ((0, xlo_ref), (1, xhi_ref)):
                h = 2 * pidx + hh
                lm = jnp.exp(jnp.where(causal, cs[:, h:h + 1] - cs_t[h:h + 1, :], -jnp.inf))
                t = jnp.dot((cbm * lm).astype(BF16), x_ref[:, sl], preferred_element_type=F32)
                yd = t if yd is None else yd + t
            st = state_ref[pidx]
            yo = jnp.dot(cg, st.astype(BF16), preferred_element_type=F32) * ecs_ref[:, sl]
            y_ref[:, sl] = yd + yo
            state_ref[pidx] = st * ecs_ref[q - 1:q, sl] + jnp.dot(bg_t, xw_ref[:, sl], preferred_element_type=F32)

    _gated_norm(y_ref, xa_ref, z_ref, dsk_ref, ng_ref, o_ref, ())

    @pl.when(c == pl.num_programs(0) - 1)
    def _():
        for p in range(SSM_HEADS // 2):
            st_ref[2 * p:2 * p + 2] = state_ref[p].T.reshape(2, SSM_HEAD_DIM, D_STATE)


def _ssm_consts(conv_w, conv_b, dt_bias, a_log, d_skip, ssm_norm_g):
    pad_h = lambda v: jnp.pad(v.reshape(1, SSM_HEADS), ((0, 0), (0, LANES - SSM_HEADS)))
    head_of_lane = np.arange(D_INNER) // SSM_HEAD_DIM
    expand = jnp.asarray((np.arange(LANES)[:, None] == head_of_lane[None, :]), BF16)
    return dict(cw=conv_w, cb=conv_b.reshape(1, CONV_DIM), dtb=pad_h(dt_bias), alog=pad_h(a_log),
                dsk=jnp.repeat(d_skip, SSM_HEAD_DIM).reshape(1, D_INNER), ng=ssm_norm_g.reshape(1, D_INNER),
                expand=expand)


def _ssd_prompt(proj, sc):
    t = proj.shape[0]
    q = SSD_CHUNK
    assert t % q == 0
    tri = jnp.asarray(np.tril(np.ones((q, q), np.float32)))
    row = lambda w, cb: pl.BlockSpec((q, w), lambda i, cb=cb: (i, cb))
    const = lambda r, w: pl.BlockSpec((r, w), lambda i: (0, 0))
    return pl.pallas_call(
        _ssd_prompt_kernel, grid=(t // q,),
        in_specs=[row(CONV_DIM, C_XBC // CONV_DIM), row(D_INNER, C_Z // D_INNER), row(MISC_W, C_MISC // MISC_W),
                  const(CONV_W, CONV_DIM), const(1, CONV_DIM), const(1, LANES), const(1, LANES),
                  const(1, D_INNER), const(1, D_INNER), const(LANES, D_INNER), const(q, q)],
        out_specs=[pl.BlockSpec((q, D_INNER), lambda i: (i, 0)),
                   pl.BlockSpec((SSM_HEADS, SSM_HEAD_DIM, D_STATE), lambda i: (0, 0, 0))],
        out_shape=[jax.ShapeDtypeStruct((t, D_INNER), BF16),
                   jax.ShapeDtypeStruct((SSM_HEADS, SSM_HEAD_DIM, D_STATE), F32)],
        scratch_shapes=[pltpu.VMEM((q + 2 * SUBLANES, CONV_DIM), F32), pltpu.VMEM((q, CONV_DIM), F32),
                        pltpu.VMEM((q, D_INNER), BF16), pltpu.VMEM((q, D_INNER), BF16), pltpu.VMEM((q, D_INNER), BF16),
                        pltpu.VMEM((q, D_INNER), F32), pltpu.VMEM((q, D_INNER), F32),
                        pltpu.VMEM((SSM_HEADS // 2, D_STATE, LANES), F32)],
        compiler_params=_cparams("arbitrary"), name="ssd_prompt",
    )(proj, proj, proj, sc["cw"], sc["cb"], sc["dtb"], sc["alog"], sc["dsk"], sc["ng"], sc["expand"], tri)


def _ssd_sample_kernel(xbc_ref, z_ref, misc_ref, cst_ref, st_ref, cw_ref, cb_ref, dtb_ref, alog_ref, dsk_ref, ng_ref,
                       exp_ref, gsum_ref, gexp_ref, o_ref, nst_ref, ext_ref, xa_ref, y_ref, *, t):
    nt = (((1,), (1,)), ((), ()))
    tn = (((0,), (0,)), ((), ()))
    ext_ref[...] = jnp.zeros_like(ext_ref)
    ext_ref[0:CONV_W - 1, :] = cst_ref[0]
    ext_ref[CONV_W - 1:CONV_W - 1 + t, :] = xbc_ref[0]
    _conv_silu(ext_ref, 0, t, cw_ref, cb_ref, xa_ref)

    dt = _softplus(misc_ref[0][:, MISC_DT:MISC_DT + LANES] + dtb_ref[...])
    da = dt * (-jnp.exp(alog_ref[...]))
    rows = [da[0:1]]
    for i in range(1, t):
        rows.append(rows[-1] + da[i:i + 1])
    cs = jnp.concatenate(rows, axis=0)
    ecs = jnp.exp(cs)
    dec_end = jnp.exp(cs[t - 1:t] - cs)
    parts = _split3(jnp.concatenate([dt, dt * dec_end, ecs, cs], axis=0))

    pairs = [(l, s) for l in range(t) for s in range(l + 1)]
    n_b = SSM_GROUPS * D_STATE
    bm = xa_ref[:, D_INNER:D_INNER + n_b]
    cm = xa_ref[:, D_INNER + n_b:D_INNER + 2 * n_b]
    prod = jnp.concatenate([cm[l:l + 1] * bm[s:s + 1] for l, s in pairs], axis=0)
    cb_parts = _split3(_dot3(_split3(prod), gsum_ref[...]))

    def stack(parts3):
        r = parts3[0].shape[0]
        rp = -(-r // (2 * SUBLANES)) * (2 * SUBLANES)
        padded = [jnp.concatenate([p, jnp.zeros((rp - r, p.shape[1]), BF16)], axis=0) if rp > r else p
                  for p in parts3]
        return jnp.concatenate(padded, axis=0), r, rp

    def dot_stacked(stacked, e):
        lhs, r, rp = stacked
        d = jnp.dot(lhs, e, preferred_element_type=F32)
        return (d[0:r] + d[rp:rp + r]) + d[2 * rp:2 * rp + r]

    parts, cb_parts = stack(parts), stack(cb_parts)
    heads_per_group = SSM_HEADS // SSM_GROUPS
    for g in range(SSM_GROUPS):
        sl = slice(g * LANE_BLK, (g + 1) * LANE_BLK)
        ex = dot_stacked(parts, exp_ref[:, sl])
        cbe = dot_stacked(cb_parts, gexp_ref[:, sl])
        xs = xa_ref[:, sl]
        xdt, xw, ecs_e, cs_e = xs * ex[0:t], xs * ex[t:2 * t], ex[2 * t:3 * t], ex[3 * t:4 * t]
        yrows = []
        for l in range(t):
            acc = None
            for s in range(l + 1):
                pi = pairs.index((l, s))
                coef = cbe[pi:pi + 1] if s == l else cbe[pi:pi + 1] * jnp.exp(cs_e[l:l + 1] - cs_e[s:s + 1])
                term = coef * xdt[s:s + 1]
                acc = term if acc is None else acc + term
            yrows.append(acc)
        sg = st_ref[0, g * heads_per_group:(g + 1) * heads_per_group].reshape(LANE_BLK, D_STATE)
        cg = xa_ref[:, D_INNER + n_b + g * D_STATE:D_INNER + n_b + (g + 1) * D_STATE].astype(BF16)
        yo = lax.dot_general(cg, sg.astype(BF16), nt, preferred_element_type=F32) * ecs_e
        y_ref[:, sl] = jnp.concatenate(yrows, axis=0) + yo
        bg = xa_ref[:, D_INNER + g * D_STATE:D_INNER + (g + 1) * D_STATE]
        zpad = jnp.zeros((SUBLANES - t, LANE_BLK), F32)
        upd = lax.dot_general(jnp.concatenate([xw, zpad], axis=0).astype(BF16),
                              jnp.concatenate([bg, zpad[:, :D_STATE]], axis=0).astype(BF16), tn,
                              preferred_element_type=F32)
        for r in range(heads_per_group):
            h = g * heads_per_group + r
            nst_ref[0, h] = st_ref[0, h] * ecs[t - 1:t, h:h + 1] + upd[r * SSM_HEAD_DIM:(r + 1) * SSM_HEAD_DIM]

    _gated_norm(y_ref, xa_ref, z_ref, dsk_ref, ng_ref, o_ref, (0,))


def _ssd_sample(proj3, conv_state, ssm_state, sc):
    b, t = proj3.shape[:2]
    assert t + CONV_W - 1 <= SUBLANES
    n_b = SSM_GROUPS * D_STATE
    gsum = jnp.asarray(np.arange(n_b)[:, None] // D_STATE == np.arange(LANES)[None, :], BF16)
    gexp = jnp.asarray(np.arange(LANES)[:, None] == (np.arange(D_INNER) // LANE_BLK)[None, :], BF16)
    row = lambda w, cb: pl.BlockSpec((1, t, w), lambda i, cb=cb: (i, 0, cb))
    const = lambda r, w: pl.BlockSpec((r, w), lambda i: (0, 0))
    st_spec = pl.BlockSpec((1, SSM_HEADS, SSM_HEAD_DIM, D_STATE), lambda i: (i, 0, 0, 0))
    kern = functools.partial(_ssd_sample_kernel, t=t)
    return pl.pallas_call(
        kern, grid=(b,),
        in_specs=[row(CONV_DIM, C_XBC // CONV_DIM), row(D_INNER, C_Z // D_INNER), row(MISC_W, C_MISC // MISC_W),
                  pl.BlockSpec((1, CONV_W - 1, CONV_DIM), lambda i: (i, 0, 0)), st_spec,
                  const(CONV_W, CONV_DIM), const(1, CONV_DIM), const(1, LANES), const(1, LANES),
                  const(1, D_INNER), const(1, D_INNER), const(LANES, D_INNER), const(n_b, LANES),
                  const(LANES, D_INNER)],
        out_specs=[pl.BlockSpec((1, t, D_INNER), lambda i: (i, 0, 0)), st_spec],
        out_shape=[jax.ShapeDtypeStruct((b, t, D_INNER), BF16),
                   jax.ShapeDtypeStruct((b, SSM_HEADS, SSM_HEAD_DIM, D_STATE), F32)],
        scratch_shapes=[pltpu.VMEM((SUBLANES, CONV_DIM), F32), pltpu.VMEM((t, CONV_DIM), F32),
                        pltpu.VMEM((t, D_INNER), F32)],
        compiler_params=_cparams("parallel"), name="ssd_sample",
    )(proj3, proj3, proj3, conv_state, ssm_state, sc["cw"], sc["cb"], sc["dtb"], sc["alog"], sc["dsk"], sc["ng"],
      sc["expand"], gsum, gexp)


SEQS_PER_STEP = 8
IDX_PAGES_PER_STEP = 8
ATTN_SEQS_PER_STEP = 2
Q_SLOTS = SUBLANES
NT_DIMS = (((1,), (1,)), ((), ()))


def _scores_sample_kernel(pt_ref, *refs, pps, past, t_new, topk, idx_bits):
    del pt_ref
    gs = SEQS_PER_STEP
    page_refs = refs[:gs * pps]
    iq_ref, w_ref, ikn_ref, rep_ref, bias_ref, keys_ref = refs[gs * pps:]
    j = pl.program_id(1)
    lp = past + LANES

    def head_sum(d, w):
        s = jnp.maximum(d, 0.0) * w
        return jnp.sum(s.reshape(IDX_HEADS, Q_SLOTS, LANES), axis=0)

    for k in range(gs):
        for p in range(pps):
            d = jnp.dot(iq_ref[k], page_refs[k * pps + p][0].astype(BF16), preferred_element_type=F32)
            col = pl.multiple_of((j * pps + p) * LANES, LANES)
            keys_ref[k, :, pl.ds(col, LANES)] = _sortable_key(head_sum(d, w_ref[k]))

    @pl.when(j == pl.num_programs(1) - 1)
    def _():
        tq = lax.broadcasted_iota(I32, (Q_SLOTS, LANES), 0)
        jn = lax.broadcasted_iota(I32, (Q_SLOTS, LANES), 1)
        ok_new = jnp.logical_and(jn <= tq, jn < t_new)
        for k in range(gs):
            d = lax.dot_general(iq_ref[k], ikn_ref[k], NT_DIMS, preferred_element_type=F32)
            s = head_sum(d, w_ref[k])
            keys_ref[k, :, past:lp] = _sortable_key(jnp.where(ok_new, s, -jnp.inf))

        idx = lax.broadcasted_iota(I32, (gs, Q_SLOTS, lp), 2)
        slot = lax.broadcasted_iota(I32, (gs, Q_SLOTS, lp), 1)
        cnt = lambda hit: jnp.sum(hit.astype(I32), axis=2, keepdims=True)
        count_ge = lambda cand: cnt(keys_ref[...] >= cand)
        count_gt = lambda cand: cnt(keys_ref[...] > cand)
        count_eq_before = lambda tau, cutc: cnt(jnp.where(keys_ref[...] == tau, (idx < cutc).astype(I32), 0))
        live = lax.broadcasted_iota(I32, (gs, Q_SLOTS, 1), 1) < t_new
        tau, cut = _select_threshold(count_ge, count_gt, count_eq_before, topk, idx_bits, live)
        keys = keys_ref[...]
        sel = jnp.where(keys == tau, (idx <= cut).astype(I32), (keys > tau).astype(I32))
        new_ok = jnp.where(idx - past <= slot, (idx - past < t_new).astype(I32), 0)
        valid = jnp.where(idx < past, 1, new_ok)
        chosen = jnp.where(sel * valid > 0, 1.0, 0.0).astype(BF16).reshape(gs * Q_SLOTS, lp)
        wide = LANES * KV_HEADS
        for j in range(lp // LANES):
            rep = jnp.dot(chosen[:, j * LANES:(j + 1) * LANES], rep_ref[...], preferred_element_type=F32)
            bias_ref[:, :, j * wide:(j + 1) * wide] = jnp.where(rep > 0.5, 0.0, NEG_BIG).reshape(gs, Q_SLOTS, wide)


def _scores_sample(page_table, cache_idx_t, iq_rows, w_rows, ikn, t_new, topk):
    b, n_pages = page_table.shape
    page = cache_idx_t.shape[2]
    assert page == LANES and b % SEQS_PER_STEP == 0
    past = n_pages * page
    lp = past + LANES
    gs = SEQS_PER_STEP
    pps = math.gcd(IDX_PAGES_PER_STEP, n_pages)
    page_spec = lambda k, p: pl.BlockSpec((1, IDX_DIM, page),
                                          lambda sg, j, pt, k=k, p=p: (pt[sg * gs + k, j * pps + p], 0, 0))
    grp = lambda r, w: pl.BlockSpec((gs, r, w), lambda sg, j, pt: (sg, 0, 0))
    rows = IDX_HEADS * Q_SLOTS
    repeat_lanes = jnp.asarray(np.arange(LANES)[:, None] == np.arange(LANES * KV_HEADS)[None, :] // KV_HEADS, BF16)
    kern = functools.partial(_scores_sample_kernel, pps=pps, past=past, t_new=t_new, topk=topk,
                             idx_bits=max(1, (lp - 1).bit_length()))
    return pl.pallas_call(
        kern,
        grid_spec=pltpu.PrefetchScalarGridSpec(
            num_scalar_prefetch=1, grid=(b // gs, n_pages // pps),
            in_specs=[page_spec(k, p) for k in range(gs) for p in range(pps)] + [
                grp(rows, IDX_DIM), grp(rows, LANES), grp(LANES, IDX_DIM),
                pl.BlockSpec((LANES, LANES * KV_HEADS), lambda sg, j, pt: (0, 0))],
            out_specs=grp(Q_SLOTS, lp * KV_HEADS),
            scratch_shapes=[pltpu.VMEM((gs, Q_SLOTS, lp), I32)]),
        out_shape=jax.ShapeDtypeStruct((b, Q_SLOTS, lp * KV_HEADS), F32),
        compiler_params=_cparams("parallel", "arbitrary"), name="sample_index_select",
    )(page_table, *([cache_idx_t] * (gs * pps)), iq_rows, w_rows, ikn, repeat_lanes)


def _attn_sample_kernel(pt_ref, *refs, n_pages, sps):
    del pt_ref
    k_refs, v_refs = refs[:sps * n_pages], refs[sps * n_pages:2 * sps * n_pages]
    kn_ref, vn_ref, q_ref, bias_ref, o_ref = refs[2 * sps * n_pages:]
    scale2 = HEAD_DIM ** -0.5 * LOG2E
    rows = KV_HEADS * Q_PER_KV * Q_SLOTS
    cols = LANES * KV_HEADS
    row_head = lax.broadcasted_iota(I32, (rows, cols), 0) // (Q_PER_KV * Q_SLOTS)
    col_head = lax.broadcasted_iota(I32, (rows, cols), 1) % KV_HEADS
    own_head = jnp.where(row_head == col_head, 0.0, NEG_BIG).astype(F32)
    for s in range(sps):
        q = q_ref[s]
        new_page = lambda ref: jnp.concatenate(
            [ref[s], jnp.zeros((cols - ref.shape[1], HEAD_DIM), F32)], axis=0)
        k_pages = [r[...] for r in k_refs[s * n_pages:(s + 1) * n_pages]] + [new_page(kn_ref)]
        v_pages = [r[...] for r in v_refs[s * n_pages:(s + 1) * n_pages]] + [new_page(vn_ref)]
        sc = []
        for j, kp in enumerate(k_pages):
            bias = jnp.concatenate([bias_ref[s, :, j * cols:(j + 1) * cols]] * (rows // Q_SLOTS), axis=0)
            sc.append(lax.dot_general(q, kp.astype(BF16), NT_DIMS, preferred_element_type=F32) + (bias + own_head))
        sm = jnp.concatenate(sc, axis=1)
        m = jnp.max(sm, axis=-1, keepdims=True)
        p = jnp.exp2((sm - m) * scale2)
        l = jnp.sum(p, axis=-1, keepdims=True)
        pb = p.astype(BF16)
        acc = None
        for j, vp in enumerate(v_pages):
            t = jnp.dot(pb[:, j * cols:(j + 1) * cols], vp.astype(BF16), preferred_element_type=F32)
            acc = t if acc is None else acc + t
        o_ref[s] = (acc / l).astype(o_ref.dtype)


def _attn_sample(page_table, cache_k, cache_v, kn, vn, q_rows, bias):
    b, n_pages = page_table.shape
    sps = math.gcd(ATTN_SEQS_PER_STEP, b)
    rows = q_rows.shape[1]
    page_rows = LANES * KV_HEADS
    page_spec = lambda s, k: pl.BlockSpec((page_rows, HEAD_DIM), lambda i, pt, s=s, k=k: (pt[i * sps + s, k], 0))
    pages = [page_spec(s, k) for s in range(sps) for k in range(n_pages)]
    per_step = lambda shape: pl.BlockSpec((sps,) + shape, lambda i, pt: (i,) + (0,) * len(shape))
    kern = functools.partial(_attn_sample_kernel, n_pages=n_pages, sps=sps)
    return pl.pallas_call(
        kern,
        grid_spec=pltpu.PrefetchScalarGridSpec(
            num_scalar_prefetch=1, grid=(b // sps,),
            in_specs=pages * 2 + [per_step(kn.shape[1:]), per_step(vn.shape[1:]),
                                  per_step((rows, HEAD_DIM)), per_step((Q_SLOTS, bias.shape[-1]))],
            out_specs=per_step((rows, HEAD_DIM))),
        out_shape=jax.ShapeDtypeStruct((b, rows, HEAD_DIM), BF16),
        compiler_params=_cparams("parallel"), name="sample_attention",
    )(page_table, *([cache_k] * (sps * n_pages)), *([cache_v] * (sps * n_pages)), kn, vn, q_rows, bias)


def _pad_rows(a, axis, size):
    widths = [(0, 0)] * a.ndim
    widths[axis] = (0, size - a.shape[axis])
    return jnp.pad(a, widths)


def _trunk(x, proj, attn_o, ssm_o, w):
    m = x.shape[0]
    tm = min(1024, m)
    mixed = _mix(attn_o, ssm_o, w["pa"], w["ps"], proj, tm, 512)
    h, hn = _outproj(x, mixed, w["out"], w["norm_mlp_g"], min(512, m))
    return _mlp(hn, h, w["up"], w["down"], min(512, m), 1024)


def kernel(x_prompt, x_sample, cache_k, cache_v, cache_idx_k, state_conv, state_ssm, page_table, norm_mix_g, w_in,
           q_norm_g, k_norm_g, idx_ln_g, idx_ln_b, conv_w, conv_b, dt_bias, a_log, d_skip, ssm_norm_g, w_proj_attn,
           w_proj_ssm, w_out, norm_mlp_g, w_up, w_down):
    depth = w_in.shape[0]
    bp, seq = x_prompt.shape[:2]
    db, t_new = x_sample.shape[:2]
    assert depth == 1 and bp == 1, "single-layer, single-prompt configuration"
    n_pages = page_table.shape[1]
    page = cache_k.shape[2]
    past = n_pages * page
    nkv = KV_HEADS * HEAD_DIM
    l = 0

    w_p = _regroup_w_in(jnp.swapaxes(w_in[l], 0, 1))
    w = dict(pa=w_proj_attn[l].astype(BF16), ps=w_proj_ssm[l].astype(BF16), out=w_out[l].astype(BF16),
             up=w_up[l].astype(BF16), down=w_down[l].astype(BF16), norm_mlp_g=norm_mlp_g[l])
    sc = _ssm_consts(conv_w[l], conv_b[l], dt_bias[l], a_log[l], d_skip[l], ssm_norm_g[l])
    qg, kg = q_norm_g[l].reshape(1, HEAD_DIM), k_norm_g[l].reshape(1, HEAD_DIM)
    lg = _pad_rows(idx_ln_g[l].reshape(1, IDX_DIM), 1, LANES)
    lb = _pad_rows(idx_ln_b[l].reshape(1, IDX_DIM), 1, LANES)

    def front(x, pos):
        m = x.shape[0]
        xn = _rmsnorm(x, norm_mix_g[l], min(512, m))
        proj = _matmul(xn, w_p, min(1024, m), 1024)
        return proj, _prep(proj, _rope_tables(pos), qg, kg, lg, lb)

    xp = x_prompt.reshape(seq, D_MODEL)
    proj_p, pp_ = front(xp, jnp.arange(seq))
    attn_p = _attn_prompt(pp_, min(TOPK_MAX, seq // 4), q_norm_g[l], k_norm_g[l])
    ssm_p, state_p = _ssd_prompt(proj_p, sc)
    y_p = _trunk(xp, proj_p, attn_p, ssm_p, w)

    xs = x_sample.reshape(db * t_new, D_MODEL)
    proj_s, ps_ = front(xs, jnp.tile(past + jnp.arange(t_new), db))
    rh = Q_PER_KV
    q_rows = _pad_rows(ps_["q_bf"].reshape(db, t_new, KV_HEADS, rh, HEAD_DIM).transpose(0, 2, 3, 1, 4), 3, Q_SLOTS)
    q_rows = q_rows.reshape(db, KV_HEADS * rh * Q_SLOTS, HEAD_DIM)
    iq_rows = _pad_rows(ps_["iq_bf"].reshape(db, t_new, IDX_HEADS, IDX_DIM).transpose(0, 2, 1, 3), 2, Q_SLOTS)
    iq_rows = iq_rows.reshape(db, IDX_HEADS * Q_SLOTS, IDX_DIM)
    w_rows = _pad_rows(ps_["wT"].reshape(IDX_HEADS, db, t_new).transpose(1, 0, 2), 2, Q_SLOTS)
    w_rows = jnp.broadcast_to(w_rows.reshape(db, IDX_HEADS * Q_SLOTS, 1), (db, IDX_HEADS * Q_SLOTS, LANES))
    ikn = _pad_rows(ps_["ik_bf"].reshape(db, t_new, IDX_DIM), 1, LANES)
    kn = ps_["k_f32"].reshape(db, t_new * KV_HEADS, HEAD_DIM)
    v_s = proj_s[:, C_V:C_V + nkv]
    vn = v_s.reshape(db, t_new * KV_HEADS, HEAD_DIM)
    bias = _scores_sample(page_table, jnp.swapaxes(cache_idx_k[l], 1, 2), iq_rows, w_rows, ikn, t_new,
                          min(TOPK_MAX, (past + t_new) // 4))
    n_pool = cache_k.shape[1]
    assert page == LANES
    o_rows = _attn_sample(page_table, cache_k[l].reshape(n_pool * page * KV_HEADS, HEAD_DIM),
                          cache_v[l].reshape(n_pool * page * KV_HEADS, HEAD_DIM), kn, vn, q_rows, bias)
    attn_s = o_rows.reshape(db, KV_HEADS, rh, Q_SLOTS, HEAD_DIM)[:, :, :, :t_new].transpose(0, 3, 1, 2, 4)
    attn_s = attn_s.reshape(db * t_new, N_HEADS * HEAD_DIM)
    ssm_s, state_s = _ssd_sample(proj_s.reshape(db, t_new, N_PROJ), state_conv[l], state_ssm[l], sc)
    y_s = _trunk(xs, proj_s, attn_s, ssm_s.reshape(db * t_new, D_INNER), w)

    kv_shape = lambda b, t: (1, b, t, KV_HEADS, HEAD_DIM)
    xbc_p = proj_p[seq - (CONV_W - 1):, C_XBC:C_XBC + CONV_DIM]
    xbc_s = proj_s.reshape(db, t_new, N_PROJ)[:, :, C_XBC:C_XBC + CONV_DIM]
    conv_s = jnp.concatenate([state_conv[l], xbc_s], axis=1)[:, t_new:]
    return (y_p.reshape(bp, seq, D_MODEL), y_s.reshape(db, t_new, D_MODEL),
            pp_["k_f32"].reshape(kv_shape(bp, seq)), proj_p[:, C_V:C_V + nkv].reshape(kv_shape(bp, seq)),
            pp_["ik_f32"].reshape(1, bp, seq, IDX_DIM), xbc_p.reshape(1, bp, CONV_W - 1, CONV_DIM),
            state_p.reshape(1, bp, SSM_HEADS, SSM_HEAD_DIM, D_STATE),
            ps_["k_f32"].reshape(kv_shape(db, t_new)), v_s.reshape(kv_shape(db, t_new)),
            ps_["ik_f32"].reshape(1, db, t_new, IDX_DIM), conv_s.reshape(1, db, CONV_W - 1, CONV_DIM),
            state_s.reshape(1, db, SSM_HEADS, SSM_HEAD_DIM, D_STATE))
```

```python
import functools
import math

import jax
import jax.numpy as jnp
import numpy as np
from jax import lax
from jax.experimental import pallas as pl
from jax.experimental.pallas import tpu as pltpu

F32, BF16, I32 = jnp.float32, jnp.bfloat16, jnp.int32

D_MODEL = 2048
N_HEADS = 16
KV_HEADS = 4
HEAD_DIM = 128
Q_PER_KV = N_HEADS // KV_HEADS
IDX_HEADS = 16
IDX_DIM = 64
TOPK_MAX = 256
D_INNER = 4096
SSM_HEAD_DIM = 64
SSM_HEADS = 64
SSM_GROUPS = 8
D_STATE = 128
CONV_W = 4
CONV_DIM = D_INNER + 2 * SSM_GROUPS * D_STATE
SSD_CHUNK = 128
D_FF = 4 * D_MODEL
ROPE_THETA = 10000.0
EPS = 1e-6
SPLITS = (N_HEADS * HEAD_DIM, KV_HEADS * HEAD_DIM, KV_HEADS * HEAD_DIM, IDX_HEADS * IDX_DIM, IDX_DIM, IDX_HEADS,
          D_INNER, CONV_DIM, SSM_HEADS, D_MODEL, D_MODEL)

LANES = 128
SUBLANES = 8
VMEM_LIMIT = 56 * 1024 * 1024

C_Z = 0
C_Q = 4096
C_XBC = 6144
C_GA = 12288
C_GB = 14336
C_K = 16384
C_V = 16896
C_IQ = 17408
C_MISC = 18432
MISC_DT = 256
REGROUP_BLK = 256
MISC_W = 512
N_PROJ = 19456

INT_MIN = -(2 ** 31)
INT_MAX = 2 ** 31 - 1
NEG_INF_KEY = -2139095041
NEG_BIG = -1e30
Q_TILE = 128
KEY_CHUNK = 1024
COUNT_CHUNK = 512
LOG2E = 1.4426950408889634


def _cparams(*sem):
    return pltpu.CompilerParams(dimension_semantics=sem, vmem_limit_bytes=VMEM_LIMIT)


def _sortable_key(x):
    bits = pltpu.bitcast(x, I32)
    return bits ^ ((bits >> 31) & INT_MAX)


def _rmsnorm_kernel(x_ref, g_ref, o_ref):
    x = x_ref[...]
    ms = jnp.mean(x * x, axis=-1, keepdims=True)
    o_ref[...] = (x * lax.rsqrt(ms + EPS) * g_ref[...]).astype(o_ref.dtype)


def _rmsnorm(x, g, tm):
    m, d = x.shape
    return pl.pallas_call(
        _rmsnorm_kernel, grid=(m // tm,),
        in_specs=[pl.BlockSpec((tm, d), lambda i: (i, 0)), pl.BlockSpec((1, d), lambda i: (0, 0))],
        out_specs=pl.BlockSpec((tm, d), lambda i: (i, 0)),
        out_shape=jax.ShapeDtypeStruct((m, d), BF16),
        compiler_params=_cparams("parallel"), name="rmsnorm",
    )(x, g.reshape(1, d))


def _matmul_kernel(a_ref, w_ref, o_ref):
    o_ref[...] = jnp.dot(a_ref[...], w_ref[...], preferred_element_type=F32)


def _matmul(a, w, tm, tn):
    m, k = a.shape
    n = w.shape[1]
    return pl.pallas_call(
        _matmul_kernel, grid=(n // tn, m // tm),
        in_specs=[pl.BlockSpec((tm, k), lambda j, i: (i, 0)), pl.BlockSpec((k, tn), lambda j, i: (0, j))],
        out_specs=pl.BlockSpec((tm, tn), lambda j, i: (i, j)),
        out_shape=jax.ShapeDtypeStruct((m, n), F32),
        compiler_params=_cparams("parallel", "parallel"), name="in_proj",
    )(a, w)


def _regroup_plan():
    src = dict(zip(("q", "k", "v", "iq", "ik", "iw", "z", "xbc", "dt", "ga", "gb"),
                   [int(c) for c in np.cumsum((0,) + SPLITS[:-1])]))
    segs = [(C_Z, src["z"], D_INNER), (C_Q, src["q"], N_HEADS * HEAD_DIM), (C_XBC, src["xbc"], CONV_DIM),
            (C_GA, src["ga"], D_MODEL), (C_GB, src["gb"], D_MODEL), (C_K, src["k"], KV_HEADS * HEAD_DIM),
            (C_V, src["v"], KV_HEADS * HEAD_DIM), (C_IQ, src["iq"], IDX_HEADS * IDX_DIM),
            (C_MISC, src["ik"], IDX_DIM + IDX_HEADS), (C_MISC + MISC_DT, src["dt"], SSM_HEADS)]
    start = np.zeros((N_PROJ // REGROUP_BLK,), np.int32)
    valid = np.zeros_like(start)
    for dst, s0, n in segs:
        assert dst % REGROUP_BLK == 0
        for off in range(0, n, REGROUP_BLK):
            j = (dst + off) // REGROUP_BLK
            start[j], valid[j] = s0 + off, min(REGROUP_BLK, n - off)
    return start, valid


def _regroup_kernel(start_ref, valid_ref, a_ref, o_ref):
    del start_ref
    row = lax.broadcasted_iota(I32, a_ref.shape, 0)
    x = jnp.where(row < valid_ref[pl.program_id(0)], a_ref[...], 0.0)
    o_ref[...] = x.T.astype(o_ref.dtype)


def _regroup_w_in(w_t):
    n_src, k = w_t.shape
    start, valid = _regroup_plan()
    assert int(start.max()) + REGROUP_BLK <= n_src and all(int(s) % SUBLANES == 0 for s in start)
    start = start // SUBLANES
    return pl.pallas_call(
        _regroup_kernel,
        grid_spec=pltpu.PrefetchScalarGridSpec(
            num_scalar_prefetch=2, grid=(N_PROJ // REGROUP_BLK,),
            in_specs=[pl.BlockSpec((pl.Element(REGROUP_BLK), pl.Element(k)),
                                   lambda j, st, va: (st[j] * SUBLANES, 0))],
            out_specs=pl.BlockSpec((k, REGROUP_BLK), lambda j, st, va: (0, j))),
        out_shape=jax.ShapeDtypeStruct((k, N_PROJ), BF16),
        compiler_params=_cparams("parallel"), name="regroup_w_in",
    )(jnp.asarray(start), jnp.asarray(valid), w_t)


def _mix_kernel(a_ref, s_ref, wa_ref, ws_ref, ga_ref, gb_ref, o_ref):
    pa = jnp.dot(a_ref[...], wa_ref[...], preferred_element_type=F32)
    ps = jnp.dot(s_ref[...], ws_ref[...], preferred_element_type=F32)
    o_ref[...] = (jax.nn.sigmoid(ga_ref[...]) * pa + jax.nn.sigmoid(gb_ref[...]) * ps).astype(o_ref.dtype)


def _mix(attn_o, ssm_o, wa, ws, proj, tm, tn):
    m = attn_o.shape[0]
    return pl.pallas_call(
        _mix_kernel, grid=(D_MODEL // tn, m // tm),
        in_specs=[pl.BlockSpec((tm, N_HEADS * HEAD_DIM), lambda j, i: (i, 0)),
                  pl.BlockSpec((tm, D_INNER), lambda j, i: (i, 0)),
                  pl.BlockSpec((N_HEADS * HEAD_DIM, tn), lambda j, i: (0, j)),
                  pl.BlockSpec((D_INNER, tn), lambda j, i: (0, j)),
                  pl.BlockSpec((tm, tn), lambda j, i: (i, C_GA // tn + j)),
                  pl.BlockSpec((tm, tn), lambda j, i: (i, C_GB // tn + j))],
        out_specs=pl.BlockSpec((tm, tn), lambda j, i: (i, j)),
        out_shape=jax.ShapeDtypeStruct((m, D_MODEL), BF16),
        compiler_params=_cparams("parallel", "parallel"), name="gated_merge",
    )(attn_o, ssm_o, wa, ws, proj, proj)


def _outproj_kernel(x_ref, mix_ref, w_ref, g_ref, h_ref, hn_ref):
    h = x_ref[...] + jnp.dot(mix_ref[...], w_ref[...], preferred_element_type=F32)
    h_ref[...] = h
    ms = jnp.mean(h * h, axis=-1, keepdims=True)
    hn_ref[...] = (h * lax.rsqrt(ms + EPS) * g_ref[...]).astype(hn_ref.dtype)


def _outproj(x, mixed, w, g, tm):
    m = x.shape[0]
    return pl.pallas_call(
        _outproj_kernel, grid=(m // tm,),
        in_specs=[pl.BlockSpec((tm, D_MODEL), lambda i: (i, 0)), pl.BlockSpec((tm, D_MODEL), lambda i: (i, 0)),
                  pl.BlockSpec((D_MODEL, D_MODEL), lambda i: (0, 0)), pl.BlockSpec((1, D_MODEL), lambda i: (0, 0))],
        out_specs=[pl.BlockSpec((tm, D_MODEL), lambda i: (i, 0)), pl.BlockSpec((tm, D_MODEL), lambda i: (i, 0))],
        out_shape=[jax.ShapeDtypeStruct((m, D_MODEL), F32), jax.ShapeDtypeStruct((m, D_MODEL), BF16)],
        compiler_params=_cparams("parallel"), name="out_proj",
    )(x, mixed, w, g.reshape(1, D_MODEL))


def _mlp_kernel(hn_ref, h_ref, wu_ref, wd_ref, o_ref):
    f = pl.program_id(1)
    u = jnp.dot(hn_ref[...], wu_ref[...], preferred_element_type=F32)
    act = jnp.square(jnp.maximum(u, 0.0)).astype(BF16)
    contrib = jnp.dot(act, wd_ref[...], preferred_element_type=F32)

    @pl.when(f == 0)
    def _():
        o_ref[...] = h_ref[...] + contrib

    @pl.when(f > 0)
    def _():
        o_ref[...] += contrib


def _mlp(hn, h, wu, wd, tm, tf):
    m = h.shape[0]
    return pl.pallas_call(
        _mlp_kernel, grid=(m // tm, D_FF // tf),
        in_specs=[pl.BlockSpec((tm, D_MODEL), lambda i, f: (i, 0)), pl.BlockSpec((tm, D_MODEL), lambda i, f: (i, 0)),
                  pl.BlockSpec((D_MODEL, tf), lambda i, f: (0, f)), pl.BlockSpec((tf, D_MODEL), lambda i, f: (f, 0))],
        out_specs=pl.BlockSpec((tm, D_MODEL), lambda i, f: (i, 0)),
        out_shape=jax.ShapeDtypeStruct((m, D_MODEL), F32),
        compiler_params=_cparams("parallel", "arbitrary"), name="mlp",
    )(hn, h, wu, wd)


def _prep_kernel(q_ref, k_ref, v_ref, iq_ref, misc_ref, ca_ref, sa_ref, cb_ref, sb_ref, qg_ref, kg_ref, lg_ref,
                 lb_ref, kf_ref, kb_ref, ikf_ref, ikb_ref, qn_ref, iqn_ref, qt_ref, iqt_ref, wt_ref, vt_ref):
    ca, sa, cb, sb = ca_ref[...], sa_ref[...], cb_ref[...], sb_ref[...]
    lane = lax.broadcasted_iota(I32, (Q_TILE, LANES), 1)
    first_half = (lane % IDX_DIM) < (IDX_DIM // 2)

    def norm_rope(x, g):
        ms = jnp.mean(x * x, axis=-1, keepdims=True)
        xn = x * lax.rsqrt(ms + EPS) * g
        return xn * ca + pltpu.roll(xn, HEAD_DIM // 2, 1) * sa

    def rope_idx(x):
        rot = jnp.where(first_half, pltpu.roll(x, LANES - IDX_DIM // 2, 1), pltpu.roll(x, IDX_DIM // 2, 1))
        return x * cb + rot * sb

    for h in range(N_HEADS):
        sl = slice(h * HEAD_DIM, (h + 1) * HEAD_DIM)
        qh = norm_rope(q_ref[:, sl], qg_ref[...])
        qn_ref[:, sl] = qh.astype(BF16)
        qt_ref[sl, :] = qh.T.astype(BF16)
    for h in range(KV_HEADS):
        sl = slice(h * HEAD_DIM, (h + 1) * HEAD_DIM)
        kh = norm_rope(k_ref[:, sl], kg_ref[...])
        kf_ref[:, sl] = kh
        kb_ref[:, sl] = kh.astype(BF16)
        vt_ref[sl, :] = v_ref[:, sl].T.astype(BF16)
    for c in range(IDX_HEADS * IDX_DIM // LANES):
        sl = slice(c * LANES, (c + 1) * LANES)
        x = rope_idx(iq_ref[:, sl])
        iqn_ref[:, sl] = x.astype(BF16)
        iqt_ref[sl, :] = x.T.astype(BF16)

    m = misc_ref[:, 0:LANES]
    is_k = lane < IDX_DIM
    mu = jnp.sum(jnp.where(is_k, m, 0.0), axis=-1, keepdims=True) * (1.0 / IDX_DIM)
    xc = jnp.where(is_k, m - mu, 0.0)
    var = jnp.sum(xc * xc, axis=-1, keepdims=True) * (1.0 / IDX_DIM)
    y = rope_idx(xc * lax.rsqrt(var + EPS) * lg_ref[...] + lb_ref[...])
    ikf_ref[...] = y[:, :IDX_DIM]
    ikb_ref[...] = y[:, :IDX_DIM].astype(BF16)
    wt_ref[...] = m.T[IDX_DIM:IDX_DIM + IDX_HEADS, :] * (IDX_HEADS ** -0.5 * IDX_DIM ** -0.5)


def _prep(proj, tabs, qg, kg, lg, lb):
    t = proj.shape[0]
    tq = Q_TILE
    row = lambda w, cb: pl.BlockSpec((tq, w), lambda i, cb=cb: (i, cb))
    const = lambda w: pl.BlockSpec((1, w), lambda i: (0, 0))
    colT = lambda r: pl.BlockSpec((r, tq), lambda i: (0, i))
    nq, nkv, niq = N_HEADS * HEAD_DIM, KV_HEADS * HEAD_DIM, IDX_HEADS * IDX_DIM
    outs = pl.pallas_call(
        _prep_kernel, grid=(t // tq,),
        in_specs=[row(nq, C_Q // nq), row(nkv, C_K // nkv), row(nkv, C_V // nkv), row(niq, C_IQ // niq),
                  row(MISC_W, C_MISC // MISC_W), row(LANES, 0), row(LANES, 0), row(LANES, 0), row(LANES, 0),
                  const(LANES), const(LANES), const(LANES), const(LANES)],
        out_specs=[row(nkv, 0), row(nkv, 0), row(IDX_DIM, 0), row(IDX_DIM, 0), row(nq, 0), row(niq, 0),
                   colT(nq), colT(niq), colT(IDX_HEADS), colT(nkv)],
        out_shape=[jax.ShapeDtypeStruct((t, nkv), F32), jax.ShapeDtypeStruct((t, nkv), BF16),
                   jax.ShapeDtypeStruct((t, IDX_DIM), F32), jax.ShapeDtypeStruct((t, IDX_DIM), BF16),
                   jax.ShapeDtypeStruct((t, nq), BF16), jax.ShapeDtypeStruct((t, niq), BF16),
                   jax.ShapeDtypeStruct((nq, t), BF16), jax.ShapeDtypeStruct((niq, t), BF16),
                   jax.ShapeDtypeStruct((IDX_HEADS, t), F32), jax.ShapeDtypeStruct((nkv, t), BF16)],
        compiler_params=_cparams("parallel"), name="norm_rope_prep",
    )(proj, proj, proj, proj, proj, *tabs, qg, kg, lg, lb)
    return dict(zip(("k_f32", "k_bf", "ik_f32", "ik_bf", "q_bf", "iq_bf", "qT", "iqT", "wT", "vT"), outs))


def _rope_tables(pos):
    def tab(dim):
        half = dim // 2
        inv = ROPE_THETA ** (-jnp.arange(half, dtype=F32) / half)
        ang = pos.astype(F32)[:, None] * inv[None, :]
        c, s = jnp.cos(ang), jnp.sin(ang)
        rep = LANES // dim
        return jnp.tile(jnp.concatenate([c, c], -1), (1, rep)), jnp.tile(jnp.concatenate([-s, s], -1), (1, rep))
    ca, sa = tab(HEAD_DIM)
    cb, sb = tab(IDX_DIM)
    return ca, sa, cb, sb


def _select_threshold(count_ge, count_gt, count_eq_before, topk, idx_bits, live):
    tau = jnp.full(live.shape, INT_MIN, I32)
    cand = jnp.zeros(live.shape, I32)
    tau = jnp.where(count_ge(cand) >= topk, cand, tau)

    def bit_step(b, tau):
        cand = tau | lax.shift_left(jnp.int32(1), 30 - b)
        return jnp.where(count_ge(cand) >= topk, cand, tau)

    tau = lax.fori_loop(0, 31, bit_step, tau)
    n_gt = count_gt(tau)
    n_eq = count_ge(tau) - n_gt
    need = topk - n_gt
    tie = jnp.logical_and(jnp.logical_and(n_eq > need, tau != NEG_INF_KEY), live)
    has_tie = jnp.max(tie.astype(I32)) > 0

    def with_ties():
        def step(b, cut):
            cand = cut | lax.shift_left(jnp.int32(1), idx_bits - 1 - b)
            return jnp.where(count_eq_before(tau, cand) < need, cand, cut)
        return lax.fori_loop(0, idx_bits, step, jnp.zeros(live.shape, I32))

    cut = lax.cond(has_tie, with_ties, lambda: jnp.full(live.shape, INT_MAX, I32))
    return tau, cut


def _attn_prompt_kernel(bound_ref, ik_ref, k_ref, vt_ref, iqt_ref, wt_ref, qt_ref, o_ref, keys_ref, acc_ref, *, topk,
                        idx_bits, bounded):
    i = pl.program_id(0)
    ch = KEY_CHUNK
    n_ch = (i * Q_TILE + Q_TILE + ch - 1) // ch
    t_glob = i * Q_TILE + lax.broadcasted_iota(I32, (1, Q_TILE), 1)
    row_iota = lax.broadcasted_iota(I32, (ch, Q_TILE), 0)

    def score_chunk(c, carry):
        base = pl.multiple_of(c * ch, ch)
        ikc = ik_ref[pl.ds(base, ch), :]
        acc = jnp.zeros((ch, Q_TILE), F32)
        for h in range(0, IDX_HEADS, 2):
            pair = jnp.concatenate([iqt_ref[h * IDX_DIM:(h + 1) * IDX_DIM, :],
                                    iqt_ref[(h + 1) * IDX_DIM:(h + 2) * IDX_DIM, :]], axis=1)
            d = jnp.dot(ikc, pair, preferred_element_type=F32)
            acc = acc + jnp.maximum(d[:, :Q_TILE], 0.0) * wt_ref[h:h + 1, :]
            acc = acc + jnp.maximum(d[:, Q_TILE:], 0.0) * wt_ref[h + 1:h + 2, :]
        acc = jnp.where(base + row_iota <= t_glob, acc, -jnp.inf)
        keys_ref[pl.ds(base, ch), :] = _sortable_key(acc)
        return carry

    lax.fori_loop(0, n_ch, score_chunk, 0)

    def _count(pred):
        cnt_iota = lax.broadcasted_iota(I32, (COUNT_CHUNK, Q_TILE), 0)

        def body(c, cnt):
            for u in range(ch // COUNT_CHUNK):
                base = pl.multiple_of(c * ch + u * COUNT_CHUNK, COUNT_CHUNK)
                hit = pred(keys_ref[pl.ds(base, COUNT_CHUNK), :], base + cnt_iota).astype(I32)
                cnt = cnt + jnp.sum(hit.reshape(COUNT_CHUNK // SUBLANES, SUBLANES, Q_TILE), axis=0)
            return cnt

        cnt = lax.fori_loop(0, n_ch, body, jnp.zeros((SUBLANES, Q_TILE), I32))
        return jnp.sum(cnt, axis=0, keepdims=True)

    tau, cut = _select_threshold(
        lambda cand: _count(lambda blk, idx: blk >= cand), lambda cand: _count(lambda blk, idx: blk > cand),
        lambda tau, cut: _count(lambda blk, idx: jnp.where(blk == tau, (idx < cut).astype(I32), 0)),
        topk, idx_bits, jnp.ones((1, Q_TILE), jnp.bool_))

    offset = -bound_ref[0] if bounded else 0.0

    def bias_chunk(c, carry):
        base = pl.multiple_of(c * ch, ch)
        blk = keys_ref[pl.ds(base, ch), :]
        idx = base + row_iota
        sel = jnp.where(blk == tau, (idx <= cut).astype(I32), (blk > tau).astype(I32))
        sel = jnp.where(idx <= t_glob, sel, 0)
        keys_ref[pl.ds(base, ch), :] = pltpu.bitcast(jnp.where(sel > 0, offset, NEG_BIG).astype(F32), I32)
        return carry

    lax.fori_loop(0, n_ch, bias_chunk, 0)

    scale2 = HEAD_DIM ** -0.5 * LOG2E
    qgs = [jnp.concatenate([qt_ref[(g * Q_PER_KV + r) * HEAD_DIM:(g * Q_PER_KV + r + 1) * HEAD_DIM, :]
                            for r in range(Q_PER_KV)], axis=1) for g in range(KV_HEADS)]
    acc_ref[...] = jnp.zeros_like(acc_ref)

    def attn_chunk(c, carry):
        base = pl.multiple_of(c * ch, ch)
        bias = jnp.concatenate([pltpu.bitcast(keys_ref[pl.ds(base, ch), :], F32)] * Q_PER_KV, axis=1)
        out = []
        for g in range(KV_HEADS):
            kc = k_ref[pl.ds(base, ch), g * HEAD_DIM:(g + 1) * HEAD_DIM]
            v_t = vt_ref[g * HEAD_DIM:(g + 1) * HEAD_DIM, pl.ds(base, ch)]
            s = jnp.dot(kc, qgs[g], preferred_element_type=F32) + bias
            if bounded:
                p = jnp.exp2(s * scale2)
                acc_ref[g] += jnp.dot(v_t, p.astype(BF16), preferred_element_type=F32)
                out.append(carry[g] + jnp.sum(p, axis=0, keepdims=True))
            else:
                m, l = carry[g]
                m_new = jnp.maximum(m, jnp.max(s, axis=0, keepdims=True))
                p = jnp.exp2((s - m_new) * scale2)
                alpha = jnp.exp2((m - m_new) * scale2)
                acc_ref[g] = acc_ref[g] * alpha + jnp.dot(v_t, p.astype(BF16), preferred_element_type=F32)
                out.append((m_new, l * alpha + jnp.sum(p, axis=0, keepdims=True)))
        return tuple(out)

    l0 = jnp.zeros((1, Q_PER_KV * Q_TILE), F32)
    init = l0 if bounded else (jnp.full((1, Q_PER_KV * Q_TILE), NEG_BIG, F32), l0)
    fin = lax.fori_loop(0, n_ch, attn_chunk, (init,) * KV_HEADS)
    for g in range(KV_HEADS):
        o = acc_ref[g] / (fin[g] if bounded else fin[g][1])
        for r in range(Q_PER_KV):
            h = g * Q_PER_KV + r
            o_ref[:, h * HEAD_DIM:(h + 1) * HEAD_DIM] = o[:, r * Q_TILE:(r + 1) * Q_TILE].T.astype(o_ref.dtype)


MAX_BOUNDED_LOGIT = 40.0


def _attn_prompt_call(p, topk, bound, bounded):
    t = p["k_bf"].shape[0]
    assert t % KEY_CHUNK == 0, "prompt length must be a multiple of the key chunk"
    nq, nkv, niq = N_HEADS * HEAD_DIM, KV_HEADS * HEAD_DIM, IDX_HEADS * IDX_DIM
    full = lambda shape: pl.BlockSpec(shape, lambda i: (0, 0))
    colT = lambda r: pl.BlockSpec((r, Q_TILE), lambda i: (0, i))
    kern = functools.partial(_attn_prompt_kernel, topk=topk, idx_bits=max(1, (t - 1).bit_length()), bounded=bounded)
    return pl.pallas_call(
        kern, grid=(t // Q_TILE,),
        in_specs=[pl.BlockSpec(memory_space=pltpu.SMEM), full((t, IDX_DIM)), full((t, nkv)), full((nkv, t)),
                  colT(niq), colT(IDX_HEADS), colT(nq)],
        out_specs=pl.BlockSpec((Q_TILE, nq), lambda i: (i, 0)),
        out_shape=jax.ShapeDtypeStruct((t, nq), BF16),
        scratch_shapes=[pltpu.VMEM((t, Q_TILE), I32), pltpu.VMEM((KV_HEADS, HEAD_DIM, Q_PER_KV * Q_TILE), F32)],
        compiler_params=_cparams("arbitrary"), name="prompt_attention_bounded" if bounded else "prompt_attention",
    )(bound, p["ik_bf"], p["k_bf"], p["vT"], p["iqT"], p["wT"], p["qT"])


def _attn_prompt(p, topk, q_gain, k_gain):
    bound = (HEAD_DIM * 1.02) * jnp.max(jnp.abs(q_gain)) * jnp.max(jnp.abs(k_gain))
    bound = bound.reshape(1).astype(F32)
    return lax.cond(bound[0] * HEAD_DIM ** -0.5 <= MAX_BOUNDED_LOGIT,
                    lambda: _attn_prompt_call(p, topk, bound, True),
                    lambda: _attn_prompt_call(p, topk, bound, False))


def _softplus(x):
    return jnp.maximum(x, 0.0) + jnp.log1p(jnp.exp(-jnp.abs(x)))


def _split3(v):
    v1 = v.astype(BF16)
    r1 = v - v1.astype(F32)
    v2 = r1.astype(BF16)
    v3 = (r1 - v2.astype(F32)).astype(BF16)
    return v1, v2, v3


def _dot3(parts, e):
    return sum(jnp.dot(p, e, preferred_element_type=F32) for p in parts)


LANE_BLK = 512


def _conv_silu(ext_ref, row0, rows, cw_ref, cb_ref, xa_ref):
    for j in range(CONV_DIM // LANE_BLK):
        sl = slice(j * LANE_BLK, (j + 1) * LANE_BLK)
        acc = ext_ref[row0:row0 + rows, sl] * cw_ref[0:1, sl]
        for i in range(1, CONV_W):
            acc = acc + ext_ref[row0 + i:row0 + i + rows, sl] * cw_ref[i:i + 1, sl]
        acc = acc + cb_ref[:, sl]
        xa_ref[:, sl] = acc * jax.nn.sigmoid(acc)


def _gated_norm(y_ref, xa_ref, z_ref, dsk_ref, ng_ref, o_ref, zidx):
    for j in range(SSM_GROUPS):
        sl = slice(j * LANE_BLK, (j + 1) * LANE_BLK)
        y = y_ref[:, sl] + dsk_ref[:, sl] * xa_ref[:, sl]
        zz = z_ref[zidx + (slice(None), sl)] if zidx else z_ref[:, sl]
        y = y * (zz * jax.nn.sigmoid(zz))
        ms = jnp.mean(y * y, axis=-1, keepdims=True)
        res = (y * lax.rsqrt(ms + EPS) * ng_ref[:, sl]).astype(o_ref.dtype)
        if zidx:
            o_ref[zidx + (slice(None), sl)] = res
        else:
            o_ref[:, sl] = res


def _ssd_prompt_kernel(xbc_ref, z_ref, misc_ref, cw_ref, cb_ref, dtb_ref, alog_ref, dsk_ref, ng_ref, exp_ref, tri_ref,
                       o_ref, st_ref, ext_ref, xa_ref, xlo_ref, xhi_ref, xw_ref, ecs_ref, y_ref, state_ref):
    c = pl.program_id(0)
    q = SSD_CHUNK
    pad = SUBLANES

    @pl.when(c == 0)
    def _():
        ext_ref[0:pad, :] = jnp.zeros((pad, CONV_DIM), F32)
        state_ref[...] = jnp.zeros_like(state_ref)

    ext_ref[pad:pad + q, :] = xbc_ref[...]
    _conv_silu(ext_ref, pad - (CONV_W - 1), q, cw_ref, cb_ref, xa_ref)
    ext_ref[0:pad, :] = ext_ref[q:q + pad, :]

    dt = _softplus(misc_ref[:, MISC_DT:MISC_DT + LANES] + dtb_ref[...])
    da = dt * (-jnp.exp(alog_ref[...]))
    cs = jnp.dot(tri_ref[...], da, precision=lax.Precision.HIGHEST, preferred_element_type=F32)
    cs_t = cs.T
    ecs = jnp.exp(cs)
    dec_end = jnp.exp(cs[q - 1:q, :] - cs)
    parts = _split3(jnp.concatenate([dt, dt * dec_end, ecs], axis=0))
    lane = lax.broadcasted_iota(I32, (q, LANE_BLK), 1)
    lo = (lane % LANES) < SSM_HEAD_DIM
    for j in range(D_INNER // LANE_BLK):
        sl = slice(j * LANE_BLK, (j + 1) * LANE_BLK)
        ex = _dot3(parts, exp_ref[:, sl])
        xs = xa_ref[:, sl]
        xdt = xs * ex[0:q]
        xlo_ref[:, sl] = jnp.where(lo, xdt, 0.0).astype(BF16)
        xhi_ref[:, sl] = jnp.where(lo, 0.0, xdt).astype(BF16)
        xw_ref[:, sl] = (xs * ex[q:2 * q]).astype(BF16)
        ecs_ref[:, sl] = ex[2 * q:3 * q]

    causal = lax.broadcasted_iota(I32, (q, q), 0) >= lax.broadcasted_iota(I32, (q, q), 1)
    heads_per_group = SSM_HEADS // SSM_GROUPS
    for g in range(SSM_GROUPS):
        b0 = D_INNER + g * D_STATE
        c0 = D_INNER + SSM_GROUPS * D_STATE + g * D_STATE
        bg = xa_ref[:, b0:b0 + D_STATE]
        cg = xa_ref[:, c0:c0 + D_STATE].astype(BF16)
        cbm = lax.dot_general(cg, bg.astype(BF16), (((1,), (1,)), ((), ())), preferred_element_type=F32)
        bg_t = bg.T.astype(BF16)
        for j in range(heads_per_group // 2):
            pidx = g * (heads_per_group // 2) + j
            sl = slice(pidx * LANES, (pidx + 1) * LANES)
            yd = None
            for hh, x_ref in ((0, xlo_ref), (1, xhi_ref)):
                h = 2 * pidx + hh
                lm = jnp.exp(jnp.where(causal, cs[:, h:h + 1] - cs_t[h:h + 1, :], -jnp.inf))
                t = jnp.dot((cbm * lm).astype(BF16), x_ref[:, sl], preferred_element_type=F32)
                yd = t if yd is None else yd + t
            st = state_ref[pidx]
            yo = jnp.dot(cg, st.astype(BF16), preferred_element_type=F32) * ecs_ref[:, sl]
            y_ref[:, sl] = yd + yo
            state_ref[pidx] = st * ecs_ref[q - 1:q, sl] + jnp.dot(bg_t, xw_ref[:, sl], preferred_element_type=F32)

    _gated_norm(y_ref, xa_ref, z_ref, dsk_ref, ng_ref, o_ref, ())

    @pl.when(c == pl.num_programs(0) - 1)
    def _():
        for p in range(SSM_HEADS // 2):
            st_ref[2 * p:2 * p + 2] = state_ref[p].T.reshape(2, SSM_HEAD_DIM, D_STATE)


def _ssm_consts(conv_w, conv_b, dt_bias, a_log, d_skip, ssm_norm_g):
    pad_h = lambda v: jnp.pad(v.reshape(1, SSM_HEADS), ((0, 0), (0, LANES - SSM_HEADS)))
    head_of_lane = np.arange(D_INNER) // SSM_HEAD_DIM
    expand = jnp.asarray((np.arange(LANES)[:, None] == head_of_lane[None, :]), BF16)
    return dict(cw=conv_w, cb=conv_b.reshape(1, CONV_DIM), dtb=pad_h(dt_bias), alog=pad_h(a_log),
                dsk=jnp.repeat(d_skip, SSM_HEAD_DIM).reshape(1, D_INNER), ng=ssm_norm_g.reshape(1, D_INNER),
                expand=expand)


def _ssd_prompt(proj, sc):
    t = proj.shape[0]
    q = SSD_CHUNK
    assert t % q == 0
    tri = jnp.asarray(np.tril(np.ones((q, q), np.float32)))
    row = lambda w, cb: pl.BlockSpec((q, w), lambda i, cb=cb: (i, cb))
    const = lambda r, w: pl.BlockSpec((r, w), lambda i: (0, 0))
    return pl.pallas_call(
        _ssd_prompt_kernel, grid=(t // q,),
        in_specs=[row(CONV_DIM, C_XBC // CONV_DIM), row(D_INNER, C_Z // D_INNER), row(MISC_W, C_MISC // MISC_W),
                  const(CONV_W, CONV_DIM), const(1, CONV_DIM), const(1, LANES), const(1, LANES),
                  const(1, D_INNER), const(1, D_INNER), const(LANES, D_INNER), const(q, q)],
        out_specs=[pl.BlockSpec((q, D_INNER), lambda i: (i, 0)),
                   pl.BlockSpec((SSM_HEADS, SSM_HEAD_DIM, D_STATE), lambda i: (0, 0, 0))],
        out_shape=[jax.ShapeDtypeStruct((t, D_INNER), BF16),
                   jax.ShapeDtypeStruct((SSM_HEADS, SSM_HEAD_DIM, D_STATE), F32)],
        scratch_shapes=[pltpu.VMEM((q + 2 * SUBLANES, CONV_DIM), F32), pltpu.VMEM((q, CONV_DIM), F32),
                        pltpu.VMEM((q, D_INNER), BF16), pltpu.VMEM((q, D_INNER), BF16), pltpu.VMEM((q, D_INNER), BF16),
                        pltpu.VMEM((q, D_INNER), F32), pltpu.VMEM((q, D_INNER), F32),
                        pltpu.VMEM((SSM_HEADS // 2, D_STATE, LANES), F32)],
        compiler_params=_cparams("arbitrary"), name="ssd_prompt",
    )(proj, proj, proj, sc["cw"], sc["cb"], sc["dtb"], sc["alog"], sc["dsk"], sc["ng"], sc["expand"], tri)


def _ssd_sample_kernel(xbc_ref, z_ref, misc_ref, cst_ref, st_ref, cw_ref, cb_ref, dtb_ref, alog_ref, dsk_ref, ng_ref,
                       exp_ref, gsum_ref, gexp_ref, o_ref, nst_ref, ext_ref, xa_ref, y_ref, *, t):
    nt = (((1,), (1,)), ((), ()))
    tn = (((0,), (0,)), ((), ()))
    ext_ref[...] = jnp.zeros_like(ext_ref)
    ext_ref[0:CONV_W - 1, :] = cst_ref[0]
    ext_ref[CONV_W - 1:CONV_W - 1 + t, :] = xbc_ref[0]
    _conv_silu(ext_ref, 0, t, cw_ref, cb_ref, xa_ref)

    dt = _softplus(misc_ref[0][:, MISC_DT:MISC_DT + LANES] + dtb_ref[...])
    da = dt * (-jnp.exp(alog_ref[...]))
    rows = [da[0:1]]
    for i in range(1, t):
        rows.append(rows[-1] + da[i:i + 1])
    cs = jnp.concatenate(rows, axis=0)
    ecs = jnp.exp(cs)
    dec_end = jnp.exp(cs[t - 1:t] - cs)
    parts = _split3(jnp.concatenate([dt, dt * dec_end, ecs, cs], axis=0))

    pairs = [(l, s) for l in range(t) for s in range(l + 1)]
    n_b = SSM_GROUPS * D_STATE
    bm = xa_ref[:, D_INNER:D_INNER + n_b]
    cm = xa_ref[:, D_INNER + n_b:D_INNER + 2 * n_b]
    prod = jnp.concatenate([cm[l:l + 1] * bm[s:s + 1] for l, s in pairs], axis=0)
    cb_parts = _split3(_dot3(_split3(prod), gsum_ref[...]))

    def stack(parts3):
        r = parts3[0].shape[0]
        rp = -(-r // (2 * SUBLANES)) * (2 * SUBLANES)
        padded = [jnp.concatenate([p, jnp.zeros((rp - r, p.shape[1]), BF16)], axis=0) if rp > r else p
                  for p in parts3]
        return jnp.concatenate(padded, axis=0), r, rp

    def dot_stacked(stacked, e):
        lhs, r, rp = stacked
        d = jnp.dot(lhs, e, preferred_element_type=F32)
        return (d[0:r] + d[rp:rp + r]) + d[2 * rp:2 * rp + r]

    parts, cb_parts = stack(parts), stack(cb_parts)
    heads_per_group = SSM_HEADS // SSM_GROUPS
    for g in range(SSM_GROUPS):
        sl = slice(g * LANE_BLK, (g + 1) * LANE_BLK)
        ex = dot_stacked(parts, exp_ref[:, sl])
        cbe = dot_stacked(cb_parts, gexp_ref[:, sl])
        xs = xa_ref[:, sl]
        xdt, xw, ecs_e, cs_e = xs * ex[0:t], xs * ex[t:2 * t], ex[2 * t:3 * t], ex[3 * t:4 * t]
        yrows = []
        for l in range(t):
            acc = None
            for s in range(l + 1):
                pi = pairs.index((l, s))
                coef = cbe[pi:pi + 1] if s == l else cbe[pi:pi + 1] * jnp.exp(cs_e[l:l + 1] - cs_e[s:s + 1])
                term = coef * xdt[s:s + 1]
                acc = term if acc is None else acc + term
            yrows.append(acc)
        sg = st_ref[0, g * heads_per_group:(g + 1) * heads_per_group].reshape(LANE_BLK, D_STATE)
        cg = xa_ref[:, D_INNER + n_b + g * D_STATE:D_INNER + n_b + (g + 1) * D_STATE].astype(BF16)
        yo = lax.dot_general(cg, sg.astype(BF16), nt, preferred_element_type=F32) * ecs_e
        y_ref[:, sl] = jnp.concatenate(yrows, axis=0) + yo
        bg = xa_ref[:, D_INNER + g * D_STATE:D_INNER + (g + 1) * D_STATE]
        zpad = jnp.zeros((SUBLANES - t, LANE_BLK), F32)
        upd = lax.dot_general(jnp.concatenate([xw, zpad], axis=0).astype(BF16),
                              jnp.concatenate([bg, zpad[:, :D_STATE]], axis=0).astype(BF16), tn,
                              preferred_element_type=F32)
        for r in range(heads_per_group):
            h = g * heads_per_group + r
            nst_ref[0, h] = st_ref[0, h] * ecs[t - 1:t, h:h + 1] + upd[r * SSM_HEAD_DIM:(r + 1) * SSM_HEAD_DIM]

    _gated_norm(y_ref, xa_ref, z_ref, dsk_ref, ng_ref, o_ref, (0,))


def _ssd_sample(proj3, conv_state, ssm_state, sc):
    b, t = proj3.shape[:2]
    assert t + CONV_W - 1 <= SUBLANES
    n_b = SSM_GROUPS * D_STATE
    gsum = jnp.asarray(np.arange(n_b)[:, None] // D_STATE == np.arange(LANES)[None, :], BF16)
    gexp = jnp.asarray(np.arange(LANES)[:, None] == (np.arange(D_INNER) // LANE_BLK)[None, :], BF16)
    row = lambda w, cb: pl.BlockSpec((1, t, w), lambda i, cb=cb: (i, 0, cb))
    const = lambda r, w: pl.BlockSpec((r, w), lambda i: (0, 0))
    st_spec = pl.BlockSpec((1, SSM_HEADS, SSM_HEAD_DIM, D_STATE), lambda i: (i, 0, 0, 0))
    kern = functools.partial(_ssd_sample_kernel, t=t)
    return pl.pallas_call(
        kern, grid=(b,),
        in_specs=[row(CONV_DIM, C_XBC // CONV_DIM), row(D_INNER, C_Z // D_INNER), row(MISC_W, C_MISC // MISC_W),
                  pl.BlockSpec((1, CONV_W - 1, CONV_DIM), lambda i: (i, 0, 0)), st_spec,
                  const(CONV_W, CONV_DIM), const(1, CONV_DIM), const(1, LANES), const(1, LANES),
                  const(1, D_INNER), const(1, D_INNER), const(LANES, D_INNER), const(n_b, LANES),
                  const(LANES, D_INNER)],
        out_specs=[pl.BlockSpec((1, t, D_INNER), lambda i: (i, 0, 0)), st_spec],
        out_shape=[jax.ShapeDtypeStruct((b, t, D_INNER), BF16),
                   jax.ShapeDtypeStruct((b, SSM_HEADS, SSM_HEAD_DIM, D_STATE), F32)],
        scratch_shapes=[pltpu.VMEM((SUBLANES, CONV_DIM), F32), pltpu.VMEM((t, CONV_DIM), F32),
                        pltpu.VMEM((t, D_INNER), F32)],
        compiler_params=_cparams("parallel"), name="ssd_sample",
    )(proj3, proj3, proj3, conv_state, ssm_state, sc["cw"], sc["cb"], sc["dtb"], sc["alog"], sc["dsk"], sc["ng"],
      sc["expand"], gsum, gexp)


SEQS_PER_STEP = 8
IDX_PAGES_PER_STEP = 8
ATTN_SEQS_PER_STEP = 2
Q_SLOTS = SUBLANES
NT_DIMS = (((1,), (1,)), ((), ()))


def _scores_sample_kernel(pt_ref, *refs, pps, past, t_new, topk, idx_bits):
    del pt_ref
    gs = SEQS_PER_STEP
    page_refs = refs[:gs * pps]
    iq_ref, w_ref, ikn_ref, rep_ref, bias_ref, keys_ref = refs[gs * pps:]
    j = pl.program_id(1)
    lp = past + LANES

    def head_sum(d, w):
        s = jnp.maximum(d, 0.0) * w
        return jnp.sum(s.reshape(IDX_HEADS, Q_SLOTS, LANES), axis=0)

    for k in range(gs):
        for p in range(pps):
            d = jnp.dot(iq_ref[k], page_refs[k * pps + p][0].astype(BF16), preferred_element_type=F32)
            col = pl.multiple_of((j * pps + p) * LANES, LANES)
            keys_ref[k, :, pl.ds(col, LANES)] = _sortable_key(head_sum(d, w_ref[k]))

    @pl.when(j == pl.num_programs(1) - 1)
    def _():
        tq = lax.broadcasted_iota(I32, (Q_SLOTS, LANES), 0)
        jn = lax.broadcasted_iota(I32, (Q_SLOTS, LANES), 1)
        ok_new = jnp.logical_and(jn <= tq, jn < t_new)
        for k in range(gs):
            d = lax.dot_general(iq_ref[k], ikn_ref[k], NT_DIMS, preferred_element_type=F32)
            s = head_sum(d, w_ref[k])
            keys_ref[k, :, past:lp] = _sortable_key(jnp.where(ok_new, s, -jnp.inf))

        idx = lax.broadcasted_iota(I32, (gs, Q_SLOTS, lp), 2)
        slot = lax.broadcasted_iota(I32, (gs, Q_SLOTS, lp), 1)
        cnt = lambda hit: jnp.sum(hit.astype(I32), axis=2, keepdims=True)
        count_ge = lambda cand: cnt(keys_ref[...] >= cand)
        count_gt = lambda cand: cnt(keys_ref[...] > cand)
        count_eq_before = lambda tau, cutc: cnt(jnp.where(keys_ref[...] == tau, (idx < cutc).astype(I32), 0))
        live = lax.broadcasted_iota(I32, (gs, Q_SLOTS, 1), 1) < t_new
        tau, cut = _select_threshold(count_ge, count_gt, count_eq_before, topk, idx_bits, live)
        keys = keys_ref[...]
        sel = jnp.where(keys == tau, (idx <= cut).astype(I32), (keys > tau).astype(I32))
        new_ok = jnp.where(idx - past <= slot, (idx - past < t_new).astype(I32), 0)
        valid = jnp.where(idx < past, 1, new_ok)
        chosen = jnp.where(sel * valid > 0, 1.0, 0.0).astype(BF16).reshape(gs * Q_SLOTS, lp)
        wide = LANES * KV_HEADS
        for j in range(lp // LANES):
            rep = jnp.dot(chosen[:, j * LANES:(j + 1) * LANES], rep_ref[...], preferred_element_type=F32)
            bias_ref[:, :, j * wide:(j + 1) * wide] = jnp.where(rep > 0.5, 0.0, NEG_BIG).reshape(gs, Q_SLOTS, wide)


def _scores_sample(page_table, cache_idx_t, iq_rows, w_rows, ikn, t_new, topk):
    b, n_pages = page_table.shape
    page = cache_idx_t.shape[2]
    assert page == LANES and b % SEQS_PER_STEP == 0
    past = n_pages * page
    lp = past + LANES
    gs = SEQS_PER_STEP
    pps = math.gcd(IDX_PAGES_PER_STEP, n_pages)
    page_spec = lambda k, p: pl.BlockSpec((1, IDX_DIM, page),
                                          lambda sg, j, pt, k=k, p=p: (pt[sg * gs + k, j * pps + p], 0, 0))
    grp = lambda r, w: pl.BlockSpec((gs, r, w), lambda sg, j, pt: (sg, 0, 0))
    rows = IDX_HEADS * Q_SLOTS
    repeat_lanes = jnp.asarray(np.arange(LANES)[:, None] == np.arange(LANES * KV_HEADS)[None, :] // KV_HEADS, BF16)
    kern = functools.partial(_scores_sample_kernel, pps=pps, past=past, t_new=t_new, topk=topk,
                             idx_bits=max(1, (lp - 1).bit_length()))
    return pl.pallas_call(
        kern,
        grid_spec=pltpu.PrefetchScalarGridSpec(
            num_scalar_prefetch=1, grid=(b // gs, n_pages // pps),
            in_specs=[page_spec(k, p) for k in range(gs) for p in range(pps)] + [
                grp(rows, IDX_DIM), grp(rows, LANES), grp(LANES, IDX_DIM),
                pl.BlockSpec((LANES, LANES * KV_HEADS), lambda sg, j, pt: (0, 0))],
            out_specs=grp(Q_SLOTS, lp * KV_HEADS),
            scratch_shapes=[pltpu.VMEM((gs, Q_SLOTS, lp), I32)]),
        out_shape=jax.ShapeDtypeStruct((b, Q_SLOTS, lp * KV_HEADS), F32),
        compiler_params=_cparams("parallel", "arbitrary"), name="sample_index_select",
    )(page_table, *([cache_idx_t] * (gs * pps)), iq_rows, w_rows, ikn, repeat_lanes)


def _attn_sample_kernel(pt_ref, *refs, n_pages, sps):
    del pt_ref
    k_refs, v_refs = refs[:sps * n_pages], refs[sps * n_pages:2 * sps * n_pages]
    kn_ref, vn_ref, q_ref, bias_ref, o_ref = refs[2 * sps * n_pages:]
    scale2 = HEAD_DIM ** -0.5 * LOG2E
    rows = KV_HEADS * Q_PER_KV * Q_SLOTS
    cols = LANES * KV_HEADS
    row_head = lax.broadcasted_iota(I32, (rows, cols), 0) // (Q_PER_KV * Q_SLOTS)
    col_head = lax.broadcasted_iota(I32, (rows, cols), 1) % KV_HEADS
    own_head = jnp.where(row_head == col_head, 0.0, NEG_BIG).astype(F32)
    for s in range(sps):
        q = q_ref[s]
        new_page = lambda ref: jnp.concatenate(
            [ref[s], jnp.zeros((cols - ref.shape[1], HEAD_DIM), F32)], axis=0)
        k_pages = [r[...] for r in k_refs[s * n_pages:(s + 1) * n_pages]] + [new_page(kn_ref)]
        v_pages = [r[...] for r in v_refs[s * n_pages:(s + 1) * n_pages]] + [new_page(vn_ref)]
        sc = []
        for j, kp in enumerate(k_pages):
            bias = jnp.concatenate([bias_ref[s, :, j * cols:(j + 1) * cols]] * (rows // Q_SLOTS), axis=0)
            sc.append(lax.dot_general(q, kp.astype(BF16), NT_DIMS, preferred_element_type=F32) + (bias + own_head))
        sm = jnp.concatenate(sc, axis=1)
        m = jnp.max(sm, axis=-1, keepdims=True)
        p = jnp.exp2((sm - m) * scale2)
        l = jnp.sum(p, axis=-1, keepdims=True)
        pb = p.astype(BF16)
        acc = None
        for j, vp in enumerate(v_pages):
            t = jnp.dot(pb[:, j * cols:(j + 1) * cols], vp.astype(BF16), preferred_element_type=F32)
            acc = t if acc is None else acc + t
        o_ref[s] = (acc / l).astype(o_ref.dtype)


def _attn_sample(page_table, cache_k, cache_v, kn, vn, q_rows, bias):
    b, n_pages = page_table.shape
    sps = math.gcd(ATTN_SEQS_PER_STEP, b)
    rows = q_rows.shape[1]
    page_rows = LANES * KV_HEADS
    page_spec = lambda s, k: pl.BlockSpec((page_rows, HEAD_DIM), lambda i, pt, s=s, k=k: (pt[i * sps + s, k], 0))
    pages = [page_spec(s, k) for s in range(sps) for k in range(n_pages)]
    per_step = lambda shape: pl.BlockSpec((sps,) + shape, lambda i, pt: (i,) + (0,) * len(shape))
    kern = functools.partial(_attn_sample_kernel, n_pages=n_pages, sps=sps)
    return pl.pallas_call(
        kern,
        grid_spec=pltpu.PrefetchScalarGridSpec(
            num_scalar_prefetch=1, grid=(b // sps,),
            in_specs=pages * 2 + [per_step(kn.shape[1:]), per_step(vn.shape[1:]),
                                  per_step((rows, HEAD_DIM)), per_step((Q_SLOTS, bias.shape[-1]))],
            out_specs=per_step((rows, HEAD_DIM))),
        out_shape=jax.ShapeDtypeStruct((b, rows, HEAD_DIM), BF16),
        compiler_params=_cparams("parallel"), name="sample_attention",
    )(page_table, *([cache_k] * (sps * n_pages)), *([cache_v] * (sps * n_pages)), kn, vn, q_rows, bias)


def _pad_rows(a, axis, size):
    widths = [(0, 0)] * a.ndim
    widths[axis] = (0, size - a.shape[axis])
    return jnp.pad(a, widths)


def _trunk(x, proj, attn_o, ssm_o, w):
    m = x.shape[0]
    tm = min(1024, m)
    mixed = _mix(attn_o, ssm_o, w["pa"], w["ps"], proj, tm, 512)
    h, hn = _outproj(x, mixed, w["out"], w["norm_mlp_g"], min(512, m))
    return _mlp(hn, h, w["up"], w["down"], min(512, m), 1024)


def kernel(x_prompt, x_sample, cache_k, cache_v, cache_idx_k, state_conv, state_ssm, page_table, norm_mix_g, w_in,
           q_norm_g, k_norm_g, idx_ln_g, idx_ln_b, conv_w, conv_b, dt_bias, a_log, d_skip, ssm_norm_g, w_proj_attn,
           w_proj_ssm, w_out, norm_mlp_g, w_up, w_down):
    depth = w_in.shape[0]
    bp, seq = x_prompt.shape[:2]
    db, t_new = x_sample.shape[:2]
    assert depth == 1 and bp == 1, "single-layer, single-prompt configuration"
    n_pages = page_table.shape[1]
    page = cache_k.shape[2]
    past = n_pages * page
    nkv = KV_HEADS * HEAD_DIM
    l = 0

    w_p = _regroup_w_in(jnp.swapaxes(w_in[l], 0, 1))
    w = dict(pa=w_proj_attn[l].astype(BF16), ps=w_proj_ssm[l].astype(BF16), out=w_out[l].astype(BF16),
             up=w_up[l].astype(BF16), down=w_down[l].astype(BF16), norm_mlp_g=norm_mlp_g[l])
    sc = _ssm_consts(conv_w[l], conv_b[l], dt_bias[l], a_log[l], d_skip[l], ssm_norm_g[l])
    qg, kg = q_norm_g[l].reshape(1, HEAD_DIM), k_norm_g[l].reshape(1, HEAD_DIM)
    lg = _pad_rows(idx_ln_g[l].reshape(1, IDX_DIM), 1, LANES)
    lb = _pad_rows(idx_ln_b[l].reshape(1, IDX_DIM), 1, LANES)

    def front(x, pos):
        m = x.shape[0]
        xn = _rmsnorm(x, norm_mix_g[l], min(512, m))
        proj = _matmul(xn, w_p, min(2048, m), 1024)
        return proj, _prep(proj, _rope_tables(pos), qg, kg, lg, lb)

    xp = x_prompt.reshape(seq, D_MODEL)
    proj_p, pp_ = front(xp, jnp.arange(seq))
    attn_p = _attn_prompt(pp_, min(TOPK_MAX, seq // 4), q_norm_g[l], k_norm_g[l])
    ssm_p, state_p = _ssd_prompt(proj_p, sc)
    y_p = _trunk(xp, proj_p, attn_p, ssm_p, w)

    xs = x_sample.reshape(db * t_new, D_MODEL)
    proj_s, ps_ = front(xs, jnp.tile(past + jnp.arange(t_new), db))
    rh = Q_PER_KV
    q_rows = _pad_rows(ps_["q_bf"].reshape(db, t_new, KV_HEADS, rh, HEAD_DIM).transpose(0, 2, 3, 1, 4), 3, Q_SLOTS)
    q_rows = q_rows.reshape(db, KV_HEADS * rh * Q_SLOTS, HEAD_DIM)
    iq_rows = _pad_rows(ps_["iq_bf"].reshape(db, t_new, IDX_HEADS, IDX_DIM).transpose(0, 2, 1, 3), 2, Q_SLOTS)
    iq_rows = iq_rows.reshape(db, IDX_HEADS * Q_SLOTS, IDX_DIM)
    w_rows = _pad_rows(ps_["wT"].reshape(IDX_HEADS, db, t_new).transpose(1, 0, 2), 2, Q_SLOTS)
    w_rows = jnp.broadcast_to(w_rows.reshape(db, IDX_HEADS * Q_SLOTS, 1), (db, IDX_HEADS * Q_SLOTS, LANES))
    ikn = _pad_rows(ps_["ik_bf"].reshape(db, t_new, IDX_DIM), 1, LANES)
    kn = ps_["k_f32"].reshape(db, t_new * KV_HEADS, HEAD_DIM)
    v_s = proj_s[:, C_V:C_V + nkv]
    vn = v_s.reshape(db, t_new * KV_HEADS, HEAD_DIM)
    bias = _scores_sample(page_table, jnp.swapaxes(cache_idx_k[l], 1, 2), iq_rows, w_rows, ikn, t_new,
                          min(TOPK_MAX, (past + t_new) // 4))
    n_pool = cache_k.shape[1]
    assert page == LANES
    o_rows = _attn_sample(page_table, cache_k[l].reshape(n_pool * page * KV_HEADS, HEAD_DIM),
                          cache_v[l].reshape(n_pool * page * KV_HEADS, HEAD_DIM), kn, vn, q_rows, bias)
    attn_s = o_rows.reshape(db, KV_HEADS, rh, Q_SLOTS, HEAD_DIM)[:, :, :, :t_new].transpose(0, 3, 1, 2, 4)
    attn_s = attn_s.reshape(db * t_new, N_HEADS * HEAD_DIM)
    ssm_s, state_s = _ssd_sample(proj_s.reshape(db, t_new, N_PROJ), state_conv[l], state_ssm[l], sc)
    y_s = _trunk(xs, proj_s, attn_s, ssm_s.reshape(db * t_new, D_INNER), w)

    kv_shape = lambda b, t: (1, b, t, KV_HEADS, HEAD_DIM)
    xbc_p = proj_p[seq - (CONV_W - 1):, C_XBC:C_XBC + CONV_DIM]
    xbc_s = proj_s.reshape(db, t_new, N_PROJ)[:, :, C_XBC:C_XBC + CONV_DIM]
    conv_s = jnp.concatenate([state_conv[l], xbc_s], axis=1)[:, t_new:]
    return (y_p.reshape(bp, seq, D_MODEL), y_s.reshape(db, t_new, D_MODEL),
            pp_["k_f32"].reshape(kv_shape(bp, seq)), proj_p[:, C_V:C_V + nkv].reshape(kv_shape(bp, seq)),
            pp_["ik_f32"].reshape(1, bp, seq, IDX_DIM), xbc_p.reshape(1, bp, CONV_W - 1, CONV_DIM),
            state_p.reshape(1, bp, SSM_HEADS, SSM_HEAD_DIM, D_STATE),
            ps_["k_f32"].reshape(kv_shape(db, t_new)), v_s.reshape(kv_shape(db, t_new)),
            ps_["ik_f32"].reshape(1, db, t_new, IDX_DIM), conv_s.reshape(1, db, CONV_W - 1, CONV_DIM),
            state_s.reshape(1, db, SSM_HEADS, SSM_HEAD_DIM, D_STATE))
```
